```python
import math
import jax, jax.numpy as jnp
from jax import lax
import numpy as np

D_MODEL = 2048
BATCH = 16
SEQ = 2048
DEPTH = 2

CHUNK = 64
Q_BLOCK = 128
MIX_WIDTH = D_MODEL
DA_HEADS = 4
DA_QK = 64
DA_V = 2 * DA_QK
DA_WIDTH = DA_HEADS * DA_V
GLA_HEADS = 4
GLA_DK = 128
GLA_DV = 256
GLA_GATE_RANK = 16
GLA_GATE_NORM = 16.0
GLA_WIDTH = GLA_HEADS * GLA_DV
SB_HEADS = 8
SB_DIM = 64
SB_WIDTH = SB_HEADS * SB_DIM
ROPE_THETA = 500000.0
ROPE_DIMS = DA_QK // 4
D_FF = 5632
PLE_DIM = 256
EPS = 1e-6

PROJ_SIZES = (
    DA_HEADS * 2 * DA_QK, DA_HEADS * 2 * DA_QK, DA_WIDTH,
    GLA_HEADS * GLA_DK, GLA_HEADS * GLA_DK, GLA_WIDTH, GLA_GATE_RANK,
    GLA_WIDTH,
    SB_WIDTH, SB_WIDTH, SB_WIDTH,
)
PROJ_WIDTH = sum(PROJ_SIZES)

kernel_name = 'hybrid_parallel_heads_streaming_encoder'


def rmsnorm(x, g):
    xf = x.astype(jnp.float32)
    y = xf * lax.rsqrt(jnp.mean(xf * xf, axis=-1, keepdims=True) + EPS)
    return (y * g.astype(jnp.float32)).astype(x.dtype)


def swiglu(x, wg, wu, wd):
    return (jax.nn.silu(x @ wg) * (x @ wu)) @ wd


def split_proj(u):
    out = []
    start = 0
    for size in PROJ_SIZES:
        out.append(u[..., start:start + size])
        start += size
    return out


def apply_partial_rope(x, cos, sin):
    half = ROPE_DIMS // 2
    x1 = x[..., :half].astype(jnp.float32)
    x2 = x[..., half:ROPE_DIMS].astype(jnp.float32)
    rot = jnp.concatenate([x1 * cos - x2 * sin, x2 * cos + x1 * sin], axis=-1).astype(x.dtype)
    return jnp.concatenate([rot, x[..., ROPE_DIMS:]], axis=-1)


def diff_attention(qa, ka, va, lam, lam_init, g_sub, cos, sin):
    B, S = qa.shape[:2]
    nb = S // Q_BLOCK
    q = jnp.transpose(apply_partial_rope(qa, cos, sin), (0, 2, 3, 1, 4))
    k = jnp.transpose(apply_partial_rope(ka, cos, sin), (0, 2, 3, 1, 4))
    v = jnp.transpose(va, (0, 2, 1, 3))
    q_blocks = jnp.moveaxis(q.reshape(B, DA_HEADS, 2, nb, Q_BLOCK, DA_QK), 3, 0)
    key_chunk = jnp.arange(S) // CHUNK
    scale = DA_QK ** -0.5

    def block(args):
        qblk, bi = args
        q_chunk = (bi * Q_BLOCK + jnp.arange(Q_BLOCK)) // CHUNK
        mask = key_chunk[None, :] <= q_chunk[:, None]
        s = jnp.einsum('bhiqd,bhikd->bhiqk', qblk, k).astype(jnp.float32) * scale
        s = jnp.where(mask, s, -jnp.inf)
        pr = jax.nn.softmax(s, axis=-1)
        w = pr[:, :, 0] - lam * pr[:, :, 1]
        return jnp.einsum('bhqk,bhkv->bhqv', w.astype(v.dtype), v)

    o = lax.map(block, (q_blocks, jnp.arange(nb)))
    o = jnp.moveaxis(o, 0, 2).reshape(B, DA_HEADS, S, DA_V)
    o = rmsnorm(o, g_sub) * (1.0 - lam_init)
    return jnp.transpose(o, (0, 2, 1, 3)).reshape(B, S, DA_WIDTH).astype(va.dtype)


def gated_linear_attention(qb, kb, vb, gate_low, w_gate_up, b_gate, og, g_norm):
    B, S = qb.shape[:2]
    n = S // CHUNK

    def heads(t, d):
        return t.reshape(B, n, CHUNK, GLA_HEADS, d).transpose(0, 3, 1, 2, 4)

    log_a = jax.nn.log_sigmoid((gate_low @ w_gate_up + b_gate).astype(jnp.float32)) / GLA_GATE_NORM
    q = heads(qb, GLA_DK).astype(jnp.float32) * GLA_DK ** -0.5
    k = heads(kb, GLA_DK).astype(jnp.float32)
    v = heads(vb, GLA_DV).astype(jnp.float32)
    bcum = jnp.cumsum(heads(log_a, GLA_DK), axis=3)
    b_last = bcum[:, :, :, -1:, :]
    q_d = q * jnp.exp(bcum)
    k_d = k * jnp.exp(-bcum)
    causal = jnp.tril(jnp.ones((CHUNK, CHUNK), dtype=bool))
    att = jnp.where(causal, jnp.einsum('bhncd,bhnsd->bhncs', q_d, k_d), 0.0)
    o_intra = jnp.einsum('bhncs,bhnse->bhnce', att, v)
    d_state = jnp.einsum('bhncd,bhnce->bhnde', k * jnp.exp(b_last - bcum), v)
    decay = jnp.exp(b_last[:, :, :, 0, :])

    def step(state, inp):
        dec, ds = inp
        return dec[..., None] * state + ds, state

    s0 = jnp.zeros((B, GLA_HEADS, GLA_DK, GLA_DV), jnp.float32)
    _, s_in = lax.scan(step, s0, (jnp.moveaxis(decay, 2, 0), jnp.moveaxis(d_state, 2, 0)))
    s_in = jnp.moveaxis(s_in, 0, 2)
    o = o_intra + jnp.einsum('bhncd,bhnde->bhnce', q_d, s_in)
    o = o.transpose(0, 2, 3, 1, 4).reshape(B, S, GLA_HEADS, GLA_DV)
    o = rmsnorm(o, g_norm) * jax.nn.silu(og.reshape(B, S, GLA_HEADS, GLA_DV).astype(jnp.float32))
    return o.reshape(B, S, GLA_WIDTH).astype(vb.dtype)


def stick_breaking_attention(qc, kc, vc):
    B, S = qc.shape[:2]
    nb = S // Q_BLOCK
    q = qc.reshape(B, S, SB_HEADS, SB_DIM).transpose(0, 2, 1, 3)
    k = kc.reshape(B, S, SB_HEADS, SB_DIM).transpose(0, 2, 1, 3)
    v = vc.reshape(B, S, SB_HEADS, SB_DIM).transpose(0, 2, 1, 3)
    q_blocks = jnp.moveaxis(q.reshape(B, SB_HEADS, nb, Q_BLOCK, SB_DIM), 2, 0)
    key_pos = jnp.arange(S)
    scale = SB_DIM ** -0.5

    def block(args):
        qblk, bi = args
        q_pos = bi * Q_BLOCK + jnp.arange(Q_BLOCK)
        mask = key_pos[None, :] < q_pos[:, None]
        z = jnp.einsum('bhqd,bhkd->bhqk', qblk, k).astype(jnp.float32) * scale
        log_beta = jax.nn.log_sigmoid(z)
        log_1m_beta = jnp.where(mask, jax.nn.log_sigmoid(-z), 0.0)
        tail = lax.cumsum(log_1m_beta, axis=3, reverse=True) - log_1m_beta
        a = jnp.where(mask, jnp.exp(log_beta + tail), 0.0)
        return jnp.einsum('bhqk,bhkd->bhqd', a.astype(v.dtype), v)

    o = lax.map(block, (q_blocks, jnp.arange(nb)))
    o = jnp.moveaxis(o, 0, 2).reshape(B, SB_HEADS, S, SB_DIM)
    return jnp.transpose(o, (0, 2, 1, 3)).reshape(B, S, SB_WIDTH).astype(vc.dtype)


def setup_inputs(seed: int = 0) -> dict:
    key = jax.random.key(seed)
    ks = iter(jax.random.split(key, 32))
    f32 = jnp.float32

    def nrm(shape, fan_in):
        return jax.random.normal(next(ks), shape, f32) * fan_in ** -0.5

    def gain(shape):
        return 1.0 + 0.02 * jax.random.normal(next(ks), shape, f32)

    x = jax.random.normal(next(ks), (BATCH, SEQ, D_MODEL), f32)
    p = jax.random.normal(next(ks), (DEPTH, BATCH, SEQ, PLE_DIM), f32)
    start = jax.random.randint(next(ks), (BATCH,), 0, 64, dtype=jnp.int32) * CHUNK
    positions = (start[:, None] + jnp.arange(SEQ, dtype=jnp.int32)[None, :]).astype(jnp.int32)
    return {
        'x': x,
        'p': p,
        'positions': positions,
        'g_ffn1': gain((DEPTH, D_MODEL)),
        'w_ffn1_gate': nrm((DEPTH, D_MODEL, D_FF), D_MODEL),
        'w_ffn1_up': nrm((DEPTH, D_MODEL, D_FF), D_MODEL),
        'w_ffn1_down': nrm((DEPTH, D_FF, D_MODEL), D_FF),
        'g_mix': gain((DEPTH, D_MODEL)),
        'w_in': nrm((DEPTH, D_MODEL, PROJ_WIDTH), D_MODEL),
        'w_gla_gate_up': nrm((DEPTH, GLA_GATE_RANK, GLA_HEADS * GLA_DK), GLA_GATE_RANK),
        'b_gla_gate': 0.1 * jax.random.normal(next(ks), (DEPTH, GLA_HEADS * GLA_DK), f32),
        'g_gla_norm': gain((DEPTH, GLA_DV)),
        'lambda_q1': 0.1 * jax.random.normal(next(ks), (DEPTH, DA_QK), f32),
        'lambda_k1': 0.1 * jax.random.normal(next(ks), (DEPTH, DA_QK), f32),
        'lambda_q2': 0.1 * jax.random.normal(next(ks), (DEPTH, DA_QK), f32),
        'lambda_k2': 0.1 * jax.random.normal(next(ks), (DEPTH, DA_QK), f32),
        'g_diff_norm': gain((DEPTH, DA_V)),
        'w_out': nrm((DEPTH, MIX_WIDTH, D_MODEL), MIX_WIDTH),
        'g_ffn2': gain((DEPTH, D_MODEL)),
        'w_ffn2_gate': nrm((DEPTH, D_MODEL, D_FF), D_MODEL),
        'w_ffn2_up': nrm((DEPTH, D_MODEL, D_FF), D_MODEL),
        'w_ffn2_down': nrm((DEPTH, D_FF, D_MODEL), D_FF),
        'g_ple': gain((DEPTH, D_MODEL)),
        'w_ple_gate': nrm((DEPTH, D_MODEL, D_MODEL), D_MODEL),
        'w_ple_proj': nrm((DEPTH, PLE_DIM, D_MODEL), PLE_DIM),
        'g_final': gain((D_MODEL,)),
    }


def reference(x, p, positions, g_ffn1, w_ffn1_gate, w_ffn1_up, w_ffn1_down, g_mix, w_in,
              w_gla_gate_up, b_gla_gate, g_gla_norm, lambda_q1, lambda_k1, lambda_q2, lambda_k2,
              g_diff_norm, w_out, g_ffn2, w_ffn2_gate, w_ffn2_up, w_ffn2_down, g_ple, w_ple_gate,
              w_ple_proj, g_final):
    B, S, _ = x.shape
    inv_freq = ROPE_THETA ** (-jnp.arange(0, ROPE_DIMS, 2, dtype=jnp.float32) / ROPE_DIMS)
    ang = positions.astype(jnp.float32)[..., None] * inv_freq
    cos = jnp.cos(ang)[:, :, None, None, :]
    sin = jnp.sin(ang)[:, :, None, None, :]
    h = x
    for i in range(DEPTH):
        h = h + 0.5 * swiglu(rmsnorm(h, g_ffn1[i]), w_ffn1_gate[i], w_ffn1_up[i], w_ffn1_down[i])
        u = rmsnorm(h, g_mix[i]) @ w_in[i]
        qa, ka, va, qb, kb, vb, gb, ob, qc, kc, vc = split_proj(u)
        lam_init = 0.8 - 0.6 * math.exp(-0.3 * i)
        lam = (jnp.exp(jnp.sum(lambda_q1[i] * lambda_k1[i]).astype(jnp.float32))
               - jnp.exp(jnp.sum(lambda_q2[i] * lambda_k2[i]).astype(jnp.float32)) + lam_init)
        o_a = diff_attention(qa.reshape(B, S, DA_HEADS, 2, DA_QK), ka.reshape(B, S, DA_HEADS, 2, DA_QK),
                             va.reshape(B, S, DA_HEADS, DA_V), lam, lam_init, g_diff_norm[i], cos, sin)
        o_b = gated_linear_attention(qb, kb, vb, gb, w_gla_gate_up[i], b_gla_gate[i], ob, g_gla_norm[i])
        o_c = stick_breaking_attention(qc, kc, vc)
        mixed = jnp.concatenate([o_a, o_b, o_c], axis=-1).astype(h.dtype)
        h = h + mixed @ w_out[i]
        h = h + 0.5 * swiglu(rmsnorm(h, g_ffn2[i]), w_ffn2_gate[i], w_ffn2_up[i], w_ffn2_down[i])
        gate = jax.nn.sigmoid(rmsnorm(h, g_ple[i]) @ w_ple_gate[i])
        h = h + (p[i] @ w_ple_proj[i]) * gate
    return rmsnorm(h, g_final)
```

```python
import functools
import math

import jax
import jax.numpy as jnp
from jax import lax
from jax.experimental import pallas as pl
from jax.experimental.pallas import tpu as pltpu

F32 = jnp.float32
BF16 = jnp.bfloat16

D_MODEL = 2048
DEPTH = 2
CHUNK = 64
DA_HEADS = 4
DA_QK = 64
DA_V = 128
DA_WIDTH = 512
GLA_HEADS = 4
GLA_DK = 128
GLA_DV = 256
GLA_RANK = 16
GLA_GATE_NORM = 16.0
GLA_WIDTH = 1024
SB_HEADS = 8
SB_DIM = 64
SB_WIDTH = 512
ROPE_THETA = 500000.0
ROPE_DIMS = 16
D_FF = 5632
PLE_DIM = 256
EPS = 1e-6

LANES = 128
U_QA, U_KA, U_VA, U_QB, U_VB, U_OB, U_KB, U_QC, U_KC, U_VC = (
    0, 512, 1024, 1536, 2048, 3072, 4096, 4608, 5120, 5632)
U_WIDTH = 6144

VMEM_LIMIT = 56 * 1024 * 1024

_NT = (((1,), (1,)), ((), ()))
_TN = (((0,), (0,)), ((), ()))


def _params(*sem):
    return pltpu.CompilerParams(dimension_semantics=sem, vmem_limit_bytes=VMEM_LIMIT)


def _rms(x, g):
    return x * lax.rsqrt(jnp.mean(x * x, axis=-1, keepdims=True) + EPS) * g


def _log_sigmoid(z):
    return jnp.minimum(z, 0.0) - jnp.log(1.0 + jnp.exp(-jnp.abs(z)))


def _sigmoid(z):
    return 1.0 / (1.0 + jnp.exp(-z))


def _ffn_body(h_ref, g_ref, wg_ref, wu_ref, wd_ref, o_ref, n_ref):
    @pl.when(pl.program_id(1) == 0)
    def _():
        x = h_ref[...]
        n_ref[...] = _rms(x, g_ref[...]).astype(BF16)
        o_ref[...] = x

    n = n_ref[...]
    a = jnp.dot(n, wg_ref[...], preferred_element_type=F32)
    b = jnp.dot(n, wu_ref[...], preferred_element_type=F32)
    act = (0.5 * a * _sigmoid(a) * b).astype(BF16)
    o_ref[...] += jnp.dot(act, wd_ref[...], preferred_element_type=F32)


def _ffn(h, g, wg, wu, wd, *, tm, tf):
    m, d = h.shape
    f = wg.shape[1]
    return pl.pallas_call(
        _ffn_body,
        grid=(m // tm, f // tf),
        in_specs=[
            pl.BlockSpec((tm, d), lambda i, j: (i, 0)),
            pl.BlockSpec((1, d), lambda i, j: (0, 0)),
            pl.BlockSpec((d, tf), lambda i, j: (0, j)),
            pl.BlockSpec((d, tf), lambda i, j: (0, j)),
            pl.BlockSpec((tf, d), lambda i, j: (j, 0)),
        ],
        out_specs=pl.BlockSpec((tm, d), lambda i, j: (i, 0)),
        out_shape=jax.ShapeDtypeStruct((m, d), F32),
        scratch_shapes=[pltpu.VMEM((tm, d), BF16)],
        compiler_params=_params("parallel", "arbitrary"),
        name="ffn",
    )(h, g, wg, wu, wd)


def _inproj_body(h_ref, g_ref, w_ref, wgate_ref, u_ref, gate_ref, n_ref):
    @pl.when(pl.program_id(1) == 0)
    def _():
        n = _rms(h_ref[...], g_ref[...]).astype(BF16)
        n_ref[...] = n
        gate_ref[...] = jnp.dot(n, wgate_ref[...], preferred_element_type=F32)

    u_ref[...] = jnp.dot(n_ref[...], w_ref[...], preferred_element_type=F32).astype(BF16)


def _inproj(h, g, w, wgate, *, tm, tn):
    m, d = h.shape
    n = w.shape[1]
    return pl.pallas_call(
        _inproj_body,
        grid=(m // tm, n // tn),
        in_specs=[
            pl.BlockSpec((tm, d), lambda i, j: (i, 0)),
            pl.BlockSpec((1, d), lambda i, j: (0, 0)),
            pl.BlockSpec((d, tn), lambda i, j: (0, j)),
            pl.BlockSpec((d, LANES), lambda i, j: (0, 0)),
        ],
        out_specs=[
            pl.BlockSpec((tm, tn), lambda i, j: (i, j)),
            pl.BlockSpec((tm, LANES), lambda i, j: (i, 0)),
        ],
        out_shape=[
            jax.ShapeDtypeStruct((m, n), BF16),
            jax.ShapeDtypeStruct((m, LANES), F32),
        ],
        scratch_shapes=[pltpu.VMEM((tm, d), BF16)],
        compiler_params=_params("parallel", "arbitrary"),
        name="inproj",
    )(h, g, w, wgate)


def _rope_table_body(pos_ref, freq_ref, c_ref, s1_ref, s2_ref):
    ang = pos_ref[...] * freq_ref[...]
    c = jnp.cos(ang)
    s = jnp.sin(ang)
    dim = lax.broadcasted_iota(jnp.int32, ang.shape, 1) % DA_QK
    half = ROPE_DIMS // 2
    c_ref[...] = jnp.where(dim < ROPE_DIMS, c, 1.0)
    s1_ref[...] = jnp.where(dim < half, -s, 0.0)
    s2_ref[...] = jnp.where((dim >= half) & (dim < ROPE_DIMS), s, 0.0)


def _rope_tables(pos_lanes, freq_lanes, *, tm):
    m = pos_lanes.shape[0]
    spec = pl.BlockSpec((tm, LANES), lambda i: (i, 0))
    return pl.pallas_call(
        _rope_table_body,
        grid=(m // tm,),
        in_specs=[spec, pl.BlockSpec((1, LANES), lambda i: (0, 0))],
        out_specs=[spec, spec, spec],
        out_shape=[jax.ShapeDtypeStruct((m, LANES), F32)] * 3,
        compiler_params=_params("parallel"),
        name="rope_tables",
    )(pos_lanes, freq_lanes)


def _rope_apply_body(q_ref, k_ref, c_ref, s1_ref, s2_ref, qo_ref, ko_ref):
    c, s1, s2 = c_ref[...], s1_ref[...], s2_ref[...]
    half = ROPE_DIMS // 2

    def rot(x):
        return x * c + pltpu.roll(x, LANES - half, 1) * s1 + pltpu.roll(x, half, 1) * s2

    for hd in range(DA_HEADS):
        sl = slice(hd * LANES, (hd + 1) * LANES)
        qo_ref[:, sl] = (rot(q_ref[:, sl].astype(F32)) * DA_QK ** -0.5).astype(BF16)
        ko_ref[:, sl] = rot(k_ref[:, sl].astype(F32)).astype(BF16)


def _rope_apply(u, c, s1, s2, *, tm):
    m = u.shape[0]
    tspec = pl.BlockSpec((tm, LANES), lambda i: (i, 0))
    return pl.pallas_call(
        _rope_apply_body,
        grid=(m // tm,),
        in_specs=[
            pl.BlockSpec((tm, DA_WIDTH), lambda i: (i, U_QA // DA_WIDTH)),
            pl.BlockSpec((tm, DA_WIDTH), lambda i: (i, U_KA // DA_WIDTH)),
            tspec, tspec, tspec,
        ],
        out_specs=[pl.BlockSpec((tm, DA_WIDTH), lambda i: (i, 0))] * 2,
        out_shape=[jax.ShapeDtypeStruct((m, DA_WIDTH), BF16)] * 2,
        compiler_params=_params("parallel"),
        name="rope_apply",
    )(u, u, c, s1, s2)


def _split_lane_halves(x):
    lane = lax.broadcasted_iota(jnp.int32, x.shape, 1)
    zero = jnp.zeros_like(x)
    return jnp.concatenate(
        [jnp.where(lane < LANES // 2, x, zero), jnp.where(lane >= LANES // 2, x, zero)], axis=0)


def _diff_body(lam_ref, gsub_ref, q_ref, k_ref, v_ref, o_ref, *, tq, lam_init):
    iq = pl.program_id(2)
    qs = _split_lane_halves(q_ref[0])

    def block(kb, vb, carry, masked):
        m, l, acc = carry
        s = lax.dot_general(qs, kb, _NT, preferred_element_type=F32)
        if masked:
            row = lax.broadcasted_iota(jnp.int32, s.shape, 0) % tq
            col = lax.broadcasted_iota(jnp.int32, s.shape, 1)
            s = jnp.where(col // CHUNK <= row // CHUNK, s, -jnp.inf)
        m_new = jnp.maximum(m, jnp.max(s, axis=-1, keepdims=True))
        alpha = jnp.exp(m - m_new)
        p = jnp.exp(s - m_new)
        l = alpha * l + jnp.sum(p, axis=-1, keepdims=True)
        acc = alpha * acc + jnp.dot(p.astype(BF16), vb, preferred_element_type=F32)
        return m_new, l, acc

    def rows(ref, j):
        return ref[0, pl.ds(pl.multiple_of(j * tq, tq), tq), :]

    init = (jnp.full((2 * tq, 1), -jnp.inf, F32), jnp.zeros((2 * tq, 1), F32),
            jnp.zeros((2 * tq, DA_V), F32))
    carry = block(rows(k_ref, iq), rows(v_ref, iq), init, True)
    m, l, acc = lax.fori_loop(
        0, iq, lambda j, c: block(rows(k_ref, j), rows(v_ref, j), c, False), carry)

    lam_p = lam_ref[...]
    lam = (jnp.exp(jnp.sum(lam_p[0:1] * lam_p[1:2], axis=-1, keepdims=True))
           - jnp.exp(jnp.sum(lam_p[2:3] * lam_p[3:4], axis=-1, keepdims=True)) + lam_init)
    o = acc / l
    o = o[:tq] - lam * o[tq:]
    o_ref[0] = (_rms(o, gsub_ref[...]) * (1.0 - lam_init)).astype(BF16)


def _diff_attention(qr, kr, u, lam_p, gsub, *, tq, lam_init):
    b, s, _ = qr.shape
    return pl.pallas_call(
        functools.partial(_diff_body, tq=tq, lam_init=lam_init),
        grid=(b, DA_HEADS, s // tq),
        in_specs=[
            pl.BlockSpec((4, DA_QK), lambda bi, hi, qi: (0, 0)),
            pl.BlockSpec((1, DA_V), lambda bi, hi, qi: (0, 0)),
            pl.BlockSpec((1, tq, LANES), lambda bi, hi, qi: (bi, qi, hi)),
            pl.BlockSpec((1, s, LANES), lambda bi, hi, qi: (bi, 0, hi)),
            pl.BlockSpec((1, s, LANES), lambda bi, hi, qi: (bi, 0, U_VA // LANES + hi)),
        ],
        out_specs=pl.BlockSpec((1, tq, LANES), lambda bi, hi, qi: (bi, qi, hi)),
        out_shape=jax.ShapeDtypeStruct((b, s, DA_WIDTH), BF16),
        compiler_params=_params("parallel", "parallel", "arbitrary"),
        name="diff_attention",
    )(lam_p, gsub, qr, kr, u)


def _sb_body(q_ref, k_ref, v_ref, o_ref, *, tq):
    iq = pl.program_id(2)
    qs = _split_lane_halves(q_ref[0] * SB_DIM ** -0.5)
    kk = lax.broadcasted_iota(jnp.int32, (tq, tq), 0)
    jj = lax.broadcasted_iota(jnp.int32, (tq, tq), 1)
    later = jnp.where(kk > jj, 1.0, 0.0).astype(BF16)

    def block(kb, vb, carry, diagonal):
        c, acc = carry
        z = lax.dot_general(qs, kb, _NT, preferred_element_type=F32)
        log_beta = _log_sigmoid(z)
        log_1m = log_beta - z
        if diagonal:
            row = lax.broadcasted_iota(jnp.int32, z.shape, 0) % tq
            col = lax.broadcasted_iota(jnp.int32, z.shape, 1)
            mask = col < row
            log_1m = jnp.where(mask, log_1m, 0.0)
        hi = log_1m.astype(BF16)
        lo = (log_1m - hi.astype(F32)).astype(BF16)
        tail = (jnp.dot(hi, later, preferred_element_type=F32)
                + jnp.dot(lo, later, preferred_element_type=F32))
        a = jnp.exp(log_beta + tail + c)
        if diagonal:
            a = jnp.where(mask, a, 0.0)
        acc = acc + jnp.dot(a.astype(BF16), vb, preferred_element_type=F32)
        c = c + jnp.sum(log_1m, axis=-1, keepdims=True)
        return c, acc

    def rows(ref, j):
        return ref[0, pl.ds(pl.multiple_of(j * tq, tq), tq), :]

    init = (jnp.zeros((2 * tq, 1), F32), jnp.zeros((2 * tq, LANES), F32))
    carry = block(rows(k_ref, iq), rows(v_ref, iq), init, True)

    def step(t, c):
        j = iq - 1 - t
        return block(rows(k_ref, j), rows(v_ref, j), c, False)

    _, acc = lax.fori_loop(0, iq, step, carry)
    lane = lax.broadcasted_iota(jnp.int32, (tq, LANES), 1)
    o_ref[0] = jnp.where(lane < LANES // 2, acc[:tq], acc[tq:]).astype(BF16)


def _sb_attention(u, *, tq):
    b, s, _ = u.shape
    return pl.pallas_call(
        functools.partial(_sb_body, tq=tq),
        grid=(b, SB_WIDTH // LANES, s // tq),
        in_specs=[
            pl.BlockSpec((1, tq, LANES), lambda bi, hi, qi: (bi, qi, U_QC // LANES + hi)),
            pl.BlockSpec((1, s, LANES), lambda bi, hi, qi: (bi, 0, U_KC // LANES + hi)),
            pl.BlockSpec((1, s, LANES), lambda bi, hi, qi: (bi, 0, U_VC // LANES + hi)),
        ],
        out_specs=pl.BlockSpec((1, tq, LANES), lambda bi, hi, qi: (bi, qi, hi)),
        out_shape=jax.ShapeDtypeStruct((b, s, SB_WIDTH), BF16),
        compiler_params=_params("parallel", "parallel", "arbitrary"),
        name="sb_attention",
    )(u, u, u)


def _gla_body(gate_ref, wup_ref, bup_ref, gnorm_ref, q_ref, k_ref, v_ref, og_ref, o_ref,
              loga_ref, state_ref):
    s = q_ref.shape[1]
    pre = jnp.dot(gate_ref[0].astype(BF16), wup_ref[...], preferred_element_type=F32)
    loga_ref[...] = _log_sigmoid(pre + bup_ref[...]) * (1.0 / GLA_GATE_NORM)
    state_ref[...] = jnp.zeros_like(state_ref)

    rr = lax.broadcasted_iota(jnp.int32, (CHUNK, CHUNK), 0)
    cc = lax.broadcasted_iota(jnp.int32, (CHUNK, CHUNK), 1)
    causal = rr >= cc
    prefix = jnp.where(causal, 1.0, 0.0).astype(BF16)
    gnorm = gnorm_ref[...]

    def chunk(n, _):
        sl = pl.ds(pl.multiple_of(n * CHUNK, CHUNK), CHUNK)
        la = loga_ref[sl, :]
        hi = la.astype(BF16)
        r1 = la - hi.astype(F32)
        mid = r1.astype(BF16)
        lo = (r1 - mid.astype(F32)).astype(BF16)
        bcum = (jnp.dot(prefix, hi, preferred_element_type=F32)
                + jnp.dot(prefix, mid, preferred_element_type=F32)
                + jnp.dot(prefix, lo, preferred_element_type=F32))
        b_last = bcum[CHUNK - 1:CHUNK, :]
        q = q_ref[0, sl, :].astype(F32) * GLA_DK ** -0.5
        k = k_ref[0, sl, :].astype(F32)
        q_d = (q * jnp.exp(bcum)).astype(BF16)
        k_d = (k * jnp.exp(-bcum)).astype(BF16)
        k_l = (k * jnp.exp(b_last - bcum)).astype(BF16)
        decay = jnp.exp(b_last)
        for hd in range(GLA_HEADS):
            ks = slice(hd * GLA_DK, (hd + 1) * GLA_DK)
            vs = slice(hd * GLA_DV, (hd + 1) * GLA_DV)
            v = v_ref[0, sl, vs]
            att = lax.dot_general(q_d[:, ks], k_d[:, ks], _NT, preferred_element_type=F32)
            att = jnp.where(causal, att, 0.0).astype(BF16)
            st = state_ref[hd]
            o = (jnp.dot(att, v, preferred_element_type=F32)
                 + lax.dot_general(q_d[:, ks], st.astype(BF16), _NT, preferred_element_type=F32))
            state_ref[hd] = decay[:, ks] * st + lax.dot_general(
                v, k_l[:, ks], _TN, preferred_element_type=F32)
            og = og_ref[0, sl, vs].astype(F32)
            o_ref[0, sl, vs] = (_rms(o, gnorm) * (og * _sigmoid(og))).astype(BF16)
        return 0

    lax.fori_loop(0, s // CHUNK, chunk, 0)


def _gla(u, gate, wup, bup, gnorm):
    b, s, _ = u.shape
    hk = GLA_HEADS * GLA_DK
    return pl.pallas_call(
        _gla_body,
        grid=(b,),
        in_specs=[
            pl.BlockSpec((1, s, LANES), lambda bi: (bi, 0, 0)),
            pl.BlockSpec((LANES, hk), lambda bi: (0, 0)),
            pl.BlockSpec((1, hk), lambda bi: (0, 0)),
            pl.BlockSpec((1, GLA_DV), lambda bi: (0, 0)),
            pl.BlockSpec((1, s, hk), lambda bi: (bi, 0, U_QB // hk)),
            pl.BlockSpec((1, s, hk), lambda bi: (bi, 0, U_KB // hk)),
            pl.BlockSpec((1, s, GLA_WIDTH), lambda bi: (bi, 0, U_VB // GLA_WIDTH)),
            pl.BlockSpec((1, s, GLA_WIDTH), lambda bi: (bi, 0, U_OB // GLA_WIDTH)),
        ],
        out_specs=pl.BlockSpec((1, s, GLA_WIDTH), lambda bi: (bi, 0, 0)),
        out_shape=jax.ShapeDtypeStruct((b, s, GLA_WIDTH), BF16),
        scratch_shapes=[pltpu.VMEM((s, hk), F32), pltpu.VMEM((GLA_HEADS, GLA_DV, GLA_DK), F32)],
        compiler_params=_params("parallel"),
        name="gla",
    )(gate, wup, bup, gnorm, u, u, u, u)


def _outproj_body(h_ref, oa_ref, ob_ref, oc_ref, wa_ref, wb_ref, wc_ref, o_ref):
    o_ref[...] = (h_ref[...]
                  + jnp.dot(oa_ref[...], wa_ref[...], preferred_element_type=F32)
                  + jnp.dot(ob_ref[...], wb_ref[...], preferred_element_type=F32)
                  + jnp.dot(oc_ref[...], wc_ref[...], preferred_element_type=F32))


def _outproj(h, oa, ob, oc, wa, wb, wc, *, tm):
    m, d = h.shape

    def rows(width):
        return pl.BlockSpec((tm, width), lambda i: (i, 0))

    def whole(width):
        return pl.BlockSpec((width, d), lambda i: (0, 0))

    return pl.pallas_call(
        _outproj_body,
        grid=(m // tm,),
        in_specs=[rows(d), rows(DA_WIDTH), rows(GLA_WIDTH), rows(SB_WIDTH),
                  whole(DA_WIDTH), whole(GLA_WIDTH), whole(SB_WIDTH)],
        out_specs=rows(d),
        out_shape=jax.ShapeDtypeStruct((m, d), F32),
        compiler_params=_params("parallel"),
        name="outproj",
    )(h, oa, ob, oc, wa, wb, wc)


def _ple_body(h_ref, p_ref, g_ref, wg_ref, wp_ref, gf_ref, o_ref, *, final):
    x = h_ref[...]
    n = _rms(x, g_ref[...]).astype(BF16)
    gate = _sigmoid(jnp.dot(n, wg_ref[...], preferred_element_type=F32))
    emb = jnp.dot(p_ref[...].astype(BF16), wp_ref[...], preferred_element_type=F32)
    y = x + emb * gate
    o_ref[...] = _rms(y, gf_ref[...]) if final else y


def _ple(h, p, g, wg, wp, gf, *, tm, final):
    m, d = h.shape
    pd = p.shape[1]
    return pl.pallas_call(
        functools.partial(_ple_body, final=final),
        grid=(m // tm,),
        in_specs=[
            pl.BlockSpec((tm, d), lambda i: (i, 0)),
            pl.BlockSpec((tm, pd), lambda i: (i, 0)),
            pl.BlockSpec((1, d), lambda i: (0, 0)),
            pl.BlockSpec((d, d), lambda i: (0, 0)),
            pl.BlockSpec((pd, d), lambda i: (0, 0)),
            pl.BlockSpec((1, d), lambda i: (0, 0)),
        ],
        out_specs=pl.BlockSpec((tm, d), lambda i: (i, 0)),
        out_shape=jax.ShapeDtypeStruct((m, d), F32),
        compiler_params=_params("parallel"),
        name="ple",
    )(h, p, g, wg, wp, gf)


def _reorder_w_in(w):
    sizes = (DA_WIDTH, DA_WIDTH, DA_WIDTH, GLA_HEADS * GLA_DK, GLA_HEADS * GLA_DK, GLA_WIDTH,
             GLA_RANK, GLA_WIDTH, SB_WIDTH, SB_WIDTH, SB_WIDTH)
    parts, start = [], 0
    for size in sizes:
        parts.append(w[:, start:start + size])
        start += size
    qa, ka, va, qb, kb, vb, gb, ob, qc, kc, vc = parts
    main = jnp.concatenate([qa, ka, va, qb, vb, ob, kb, qc, kc, vc], axis=1).astype(BF16)
    gate = jnp.pad(gb, ((0, 0), (0, LANES - GLA_RANK))).astype(BF16)
    return main, gate


def kernel(x, p, positions, g_ffn1, w_ffn1_gate, w_ffn1_up, w_ffn1_down, g_mix, w_in, w_gla_gate_up, b_gla_gate, g_gla_norm, lambda_q1, lambda_k1, lambda_q2, lambda_k2, g_diff_norm, w_out, g_ffn2, w_ffn2_gate, w_ffn2_up, w_ffn2_down, g_ple, w_ple_gate, w_ple_proj, g_final):
    b, s, d = x.shape
    m = b * s
    tm = min(512, m)
    tq = min(256, s)

    inv_freq = ROPE_THETA ** (-jnp.arange(0, ROPE_DIMS, 2, dtype=F32) / ROPE_DIMS)
    freq_lanes = jnp.tile(inv_freq, LANES // inv_freq.shape[0]).reshape(1, LANES)
    pos_lanes = jnp.broadcast_to(positions.astype(F32).reshape(m, 1), (m, LANES))
    rope_c, rope_s1, rope_s2 = _rope_tables(pos_lanes, freq_lanes, tm=tm)

    h = x.reshape(m, d)
    for i in range(DEPTH):
        row = lambda a: a[i].reshape(1, -1)
        h = _ffn(h, row(g_ffn1), w_ffn1_gate[i].astype(BF16), w_ffn1_up[i].astype(BF16),
                 w_ffn1_down[i].astype(BF16), tm=tm, tf=512)

        w_main, w_gate = _reorder_w_in(w_in[i])
        u, gate_low = _inproj(h, row(g_mix), w_main, w_gate, tm=tm, tn=512)
        qr, kr = _rope_apply(u, rope_c, rope_s1, rope_s2, tm=tm)
        u3 = u.reshape(b, s, U_WIDTH)

        lam_init = 0.8 - 0.6 * math.exp(-0.3 * i)
        lam_p = jnp.stack([lambda_q1[i], lambda_k1[i], lambda_q2[i], lambda_k2[i]])
        o_a = _diff_attention(qr.reshape(b, s, DA_WIDTH), kr.reshape(b, s, DA_WIDTH), u3, lam_p,
                              row(g_diff_norm), tq=tq, lam_init=lam_init)
        wup = jnp.pad(w_gla_gate_up[i], ((0, LANES - GLA_RANK), (0, 0))).astype(BF16)
        o_b = _gla(u3, gate_low.reshape(b, s, LANES), wup, row(b_gla_gate), row(g_gla_norm))
        o_c = _sb_attention(u3, tq=tq)

        wo = w_out[i].astype(BF16)
        h = _outproj(h, o_a.reshape(m, DA_WIDTH), o_b.reshape(m, GLA_WIDTH),
                     o_c.reshape(m, SB_WIDTH), wo[:DA_WIDTH], wo[DA_WIDTH:DA_WIDTH + GLA_WIDTH],
                     wo[DA_WIDTH + GLA_WIDTH:], tm=tm)

        h = _ffn(h, row(g_ffn2), w_ffn2_gate[i].astype(BF16), w_ffn2_up[i].astype(BF16),
                 w_ffn2_down[i].astype(BF16), tm=tm, tf=512)
        h = _ple(h, p[i].reshape(m, PLE_DIM), row(g_ple), w_ple_gate[i].astype(BF16),
                 w_ple_proj[i].astype(BF16), g_final.reshape(1, -1), tm=tm,
                 final=(i == DEPTH - 1))
    return h.reshape(b, s, d)
```

```python
import functools
import math

import jax
import jax.numpy as jnp
from jax import lax
from jax.experimental import pallas as pl
from jax.experimental.pallas import tpu as pltpu

F32 = jnp.float32
BF16 = jnp.bfloat16

D_MODEL = 2048
DEPTH = 2
CHUNK = 64
DA_HEADS = 4
DA_QK = 64
DA_V = 128
DA_WIDTH = 512
GLA_HEADS = 4
GLA_DK = 128
GLA_DV = 256
GLA_RANK = 16
GLA_GATE_NORM = 16.0
GLA_WIDTH = 1024
SB_HEADS = 8
SB_DIM = 64
SB_WIDTH = 512
ROPE_THETA = 500000.0
ROPE_DIMS = 16
D_FF = 5632
PLE_DIM = 256
EPS = 1e-6

LANES = 128
U_QA, U_KA, U_VA, U_QB, U_VB, U_OB, U_KB, U_QC, U_KC, U_VC = (
    0, 512, 1024, 1536, 2048, 3072, 4096, 4608, 5120, 5632)
U_WIDTH = 6144

VMEM_LIMIT = 56 * 1024 * 1024

_NT = (((1,), (1,)), ((), ()))
_TN = (((0,), (0,)), ((), ()))


def _params(*sem):
    return pltpu.CompilerParams(dimension_semantics=sem, vmem_limit_bytes=VMEM_LIMIT)


def _rms(x, g):
    return x * lax.rsqrt(jnp.mean(x * x, axis=-1, keepdims=True) + EPS) * g


def _log_sigmoid(z):
    return jnp.minimum(z, 0.0) - jnp.log(1.0 + jnp.exp(-jnp.abs(z)))


def _sigmoid(z):
    return 1.0 / (1.0 + jnp.exp(-z))


def _ffn_body(h_ref, g_ref, wg_ref, wu_ref, wd_ref, o_ref, n_ref):
    @pl.when(pl.program_id(1) == 0)
    def _():
        x = h_ref[...]
        n_ref[...] = _rms(x, g_ref[...]).astype(BF16)
        o_ref[...] = x

    n = n_ref[...]
    a = jnp.dot(n, wg_ref[...], preferred_element_type=F32)
    b = jnp.dot(n, wu_ref[...], preferred_element_type=F32)
    act = (0.5 * a * _sigmoid(a) * b).astype(BF16)
    o_ref[...] += jnp.dot(act, wd_ref[...], preferred_element_type=F32)


def _ffn(h, g, wg, wu, wd, *, tm, tf):
    m, d = h.shape
    f = wg.shape[1]
    return pl.pallas_call(
        _ffn_body,
        grid=(m // tm, f // tf),
        in_specs=[
            pl.BlockSpec((tm, d), lambda i, j: (i, 0)),
            pl.BlockSpec((1, d), lambda i, j: (0, 0)),
            pl.BlockSpec((d, tf), lambda i, j: (0, j)),
            pl.BlockSpec((d, tf), lambda i, j: (0, j)),
            pl.BlockSpec((tf, d), lambda i, j: (j, 0)),
        ],
        out_specs=pl.BlockSpec((tm, d), lambda i, j: (i, 0)),
        out_shape=jax.ShapeDtypeStruct((m, d), F32),
        scratch_shapes=[pltpu.VMEM((tm, d), BF16)],
        compiler_params=_params("parallel", "arbitrary"),
        name="ffn",
    )(h, g, wg, wu, wd)


def _inproj_body(h_ref, g_ref, w_ref, wgate_ref, u_ref, gate_ref, n_ref):
    @pl.when(pl.program_id(1) == 0)
    def _():
        n = _rms(h_ref[...], g_ref[...]).astype(BF16)
        n_ref[...] = n
        gate_ref[...] = jnp.dot(n, wgate_ref[...], preferred_element_type=F32)

    u_ref[...] = jnp.dot(n_ref[...], w_ref[...], preferred_element_type=F32).astype(BF16)


def _inproj(h, g, w, wgate, *, tm, tn):
    m, d = h.shape
    n = w.shape[1]
    return pl.pallas_call(
        _inproj_body,
        grid=(m // tm, n // tn),
        in_specs=[
            pl.BlockSpec((tm, d), lambda i, j: (i, 0)),
            pl.BlockSpec((1, d), lambda i, j: (0, 0)),
            pl.BlockSpec((d, tn), lambda i, j: (0, j)),
            pl.BlockSpec((d, LANES), lambda i, j: (0, 0)),
        ],
        out_specs=[
            pl.BlockSpec((tm, tn), lambda i, j: (i, j)),
            pl.BlockSpec((tm, LANES), lambda i, j: (i, 0)),
        ],
        out_shape=[
            jax.ShapeDtypeStruct((m, n), BF16),
            jax.ShapeDtypeStruct((m, LANES), F32),
        ],
        scratch_shapes=[pltpu.VMEM((tm, d), BF16)],
        compiler_params=_params("parallel", "arbitrary"),
        name="inproj",
    )(h, g, w, wgate)


def _rope_table_body(pos_ref, freq_ref, c_ref, s1_ref, s2_ref):
    ang = pos_ref[...] * freq_ref[...]
    c = jnp.cos(ang)
    s = jnp.sin(ang)
    dim = lax.broadcasted_iota(jnp.int32, ang.shape, 1) % DA_QK
    half = ROPE_DIMS // 2
    c_ref[...] = jnp.where(dim < ROPE_DIMS, c, 1.0)
    s1_ref[...] = jnp.where(dim < half, -s, 0.0)
    s2_ref[...] = jnp.where((dim >= half) & (dim < ROPE_DIMS), s, 0.0)


def _rope_tables(pos_lanes, freq_lanes, *, tm):
    m = pos_lanes.shape[0]
    spec = pl.BlockSpec((tm, LANES), lambda i: (i, 0))
    return pl.pallas_call(
        _rope_table_body,
        grid=(m // tm,),
        in_specs=[spec, pl.BlockSpec((1, LANES), lambda i: (0, 0))],
        out_specs=[spec, spec, spec],
        out_shape=[jax.ShapeDtypeStruct((m, LANES), F32)] * 3,
        compiler_params=_params("parallel"),
        name="rope_tables",
    )(pos_lanes, freq_lanes)


def _rope_apply_body(q_ref, k_ref, c_ref, s1_ref, s2_ref, qo_ref, ko_ref):
    c, s1, s2 = c_ref[...], s1_ref[...], s2_ref[...]
    half = ROPE_DIMS // 2
    q_scale = DA_QK ** -0.5 * math.log2(math.e)

    def rot(x):
        return x * c + pltpu.roll(x, LANES - half, 1) * s1 + pltpu.roll(x, half, 1) * s2

    for hd in range(DA_HEADS):
        sl = slice(hd * LANES, (hd + 1) * LANES)
        qo_ref[:, sl] = (rot(q_ref[:, sl].astype(F32)) * q_scale).astype(BF16)
        ko_ref[:, sl] = rot(k_ref[:, sl].astype(F32)).astype(BF16)


def _rope_apply(u, c, s1, s2, *, tm):
    m = u.shape[0]
    tspec = pl.BlockSpec((tm, LANES), lambda i: (i, 0))
    return pl.pallas_call(
        _rope_apply_body,
        grid=(m // tm,),
        in_specs=[
            pl.BlockSpec((tm, DA_WIDTH), lambda i: (i, U_QA // DA_WIDTH)),
            pl.BlockSpec((tm, DA_WIDTH), lambda i: (i, U_KA // DA_WIDTH)),
            tspec, tspec, tspec,
        ],
        out_specs=[pl.BlockSpec((tm, DA_WIDTH), lambda i: (i, 0))] * 2,
        out_shape=[jax.ShapeDtypeStruct((m, DA_WIDTH), BF16)] * 2,
        compiler_params=_params("parallel"),
        name="rope_apply",
    )(u, u, c, s1, s2)


def _split_lane_halves(x):
    lane = lax.broadcasted_iota(jnp.int32, x.shape, 1)
    zero = jnp.zeros_like(x)
    return jnp.concatenate(
        [jnp.where(lane < LANES // 2, x, zero), jnp.where(lane >= LANES // 2, x, zero)], axis=0)


def _fold_lanes(x, op):
    out = x[:, :LANES]
    for g in range(1, x.shape[1] // LANES):
        out = op(out, x[:, g * LANES:(g + 1) * LANES])
    return out


def _diff_body(lam_ref, gsub_ref, q_ref, k_ref, v_ref, o_ref, s_ref, p_ref, *, tq, lam_init):
    nq = q_ref.shape[1] // tq
    lam_p = lam_ref[...]
    lam = (jnp.exp(jnp.sum(lam_p[0:1] * lam_p[1:2], axis=-1, keepdims=True))
           - jnp.exp(jnp.sum(lam_p[2:3] * lam_p[3:4], axis=-1, keepdims=True)) + lam_init)
    gsub = gsub_ref[...]
    row = lax.broadcasted_iota(jnp.int32, (2 * tq, tq), 0) % tq
    col = lax.broadcasted_iota(jnp.int32, (2 * tq, tq), 1)
    chunk_mask = col // CHUNK <= row // CHUNK

    for c in range(nq):
        buf = c % 2
        qs = _split_lane_halves(q_ref[0, c * tq:(c + 1) * tq, :])
        mx = None
        for j in range(c + 1):
            ks = slice(j * tq, (j + 1) * tq)
            s = lax.dot_general(qs, k_ref[0, ks, :], _NT, preferred_element_type=F32)
            if j == c:
                s = jnp.where(chunk_mask, s, -jnp.inf)
            s_ref[buf, :, ks] = s
            mj = _fold_lanes(s, jnp.maximum)
            mx = mj if mx is None else jnp.maximum(mx, mj)
        m = jnp.max(mx, axis=-1, keepdims=True)
        ls = None
        for j in range(c + 1):
            ks = slice(j * tq, (j + 1) * tq)
            p = jnp.exp2(s_ref[buf, :, ks] - m)
            p_ref[buf, :, ks] = p.astype(BF16)
            pj = _fold_lanes(p, jnp.add)
            ls = pj if ls is None else ls + pj
        l = jnp.sum(ls, axis=-1, keepdims=True)
        n = (c + 1) * tq
        acc = jnp.dot(p_ref[buf, :, :n], v_ref[0, :n, :], preferred_element_type=F32)
        o = acc / l
        o = o[:tq] - lam * o[tq:]
        o_ref[0, c * tq:(c + 1) * tq, :] = (_rms(o, gsub) * (1.0 - lam_init)).astype(BF16)


def _diff_attention(qr, kr, u, lam_p, gsub, *, tq, lam_init):
    b, s, _ = qr.shape
    return pl.pallas_call(
        functools.partial(_diff_body, tq=tq, lam_init=lam_init),
        grid=(b, DA_HEADS),
        in_specs=[
            pl.BlockSpec((4, DA_QK), lambda bi, hi: (0, 0)),
            pl.BlockSpec((1, DA_V), lambda bi, hi: (0, 0)),
            pl.BlockSpec((1, s, LANES), lambda bi, hi: (bi, 0, hi)),
            pl.BlockSpec((1, s, LANES), lambda bi, hi: (bi, 0, hi)),
            pl.BlockSpec((1, s, LANES), lambda bi, hi: (bi, 0, U_VA // LANES + hi)),
        ],
        out_specs=pl.BlockSpec((1, s, LANES), lambda bi, hi: (bi, 0, hi)),
        out_shape=jax.ShapeDtypeStruct((b, s, DA_WIDTH), BF16),
        scratch_shapes=[pltpu.VMEM((2, 2 * tq, s), F32), pltpu.VMEM((2, 2 * tq, s), BF16)],
        compiler_params=_params("parallel", "parallel"),
        name="diff_attention",
    )(lam_p, gsub, qr, kr, u)


def _sb_body(q_ref, k_ref, v_ref, o_ref, a_ref, *, tq):
    nq = q_ref.shape[1] // tq
    kk = lax.broadcasted_iota(jnp.int32, (2 * tq, tq), 0) % tq
    jj = lax.broadcasted_iota(jnp.int32, (2 * tq, tq), 1)
    later = jnp.where(kk > jj, 1.0, 0.0).astype(BF16)
    strictly_before = jj < kk
    lane = lax.broadcasted_iota(jnp.int32, (tq, LANES), 1)
    log2e = math.log2(math.e)

    for c in range(nq):
        buf = c % 2
        q = q_ref[0, c * tq:(c + 1) * tq, :].astype(F32) * (SB_DIM ** -0.5 * log2e)
        qs = _split_lane_halves(q.astype(BF16))
        carry = None
        for j in range(c, -1, -1):
            ks = slice(j * tq, (j + 1) * tq)
            z = lax.dot_general(qs, k_ref[0, ks, :], _NT, preferred_element_type=F32)
            softplus = jnp.log(1.0 + jnp.exp2(-jnp.abs(z))) * log2e
            log_beta = jnp.minimum(z, 0.0) - softplus
            log_1m = log_beta - z
            if j == c:
                log_1m = jnp.where(strictly_before, log_1m, 0.0)
            hi = log_1m.astype(BF16)
            lo = (log_1m - hi.astype(F32)).astype(BF16)
            tail = jnp.dot(jnp.concatenate([hi, lo], axis=1), later,
                           preferred_element_type=F32)
            expo = log_beta + tail
            if carry is not None:
                expo = expo + carry
            a = jnp.exp2(expo)
            if j == c:
                a = jnp.where(strictly_before, a, 0.0)
            a_ref[buf, :, ks] = a.astype(BF16)
            rs = jnp.sum(_fold_lanes(log_1m, jnp.add), axis=-1, keepdims=True)
            carry = rs if carry is None else carry + rs
        n = (c + 1) * tq
        acc = jnp.dot(a_ref[buf, :, :n], v_ref[0, :n, :], preferred_element_type=F32)
        o_ref[0, c * tq:(c + 1) * tq, :] = jnp.where(
            lane < LANES // 2, acc[:tq], acc[tq:]).astype(BF16)


def _sb_attention(u, *, tq):
    b, s, _ = u.shape
    return pl.pallas_call(
        functools.partial(_sb_body, tq=tq),
        grid=(b, SB_WIDTH // LANES),
        in_specs=[
            pl.BlockSpec((1, s, LANES), lambda bi, hi: (bi, 0, U_QC // LANES + hi)),
            pl.BlockSpec((1, s, LANES), lambda bi, hi: (bi, 0, U_KC // LANES + hi)),
            pl.BlockSpec((1, s, LANES), lambda bi, hi: (bi, 0, U_VC // LANES + hi)),
        ],
        out_specs=pl.BlockSpec((1, s, LANES), lambda bi, hi: (bi, 0, hi)),
        out_shape=jax.ShapeDtypeStruct((b, s, SB_WIDTH), BF16),
        scratch_shapes=[pltpu.VMEM((2, 2 * tq, s), BF16)],
        compiler_params=_params("parallel", "parallel"),
        name="sb_attention",
    )(u, u, u)


def _gla_body(gate_ref, wup_ref, bup_ref, gnorm_ref, q_ref, k_ref, v_ref, og_ref, o_ref,
              loga_ref, state_ref):
    s = q_ref.shape[1]
    pre = jnp.dot(gate_ref[0].astype(BF16), wup_ref[...], preferred_element_type=F32)
    loga_ref[...] = _log_sigmoid(pre + bup_ref[...]) * (1.0 / GLA_GATE_NORM)
    state_ref[...] = jnp.zeros_like(state_ref)

    rr = lax.broadcasted_iota(jnp.int32, (CHUNK, CHUNK), 0)
    cc = lax.broadcasted_iota(jnp.int32, (CHUNK, CHUNK), 1)
    causal = rr >= cc
    prefix = jnp.where(causal, 1.0, 0.0).astype(BF16)
    gnorm = gnorm_ref[...]

    def chunk(n, _):
        sl = pl.ds(pl.multiple_of(n * CHUNK, CHUNK), CHUNK)
        la = loga_ref[sl, :]
        hi = la.astype(BF16)
        r1 = la - hi.astype(F32)
        mid = r1.astype(BF16)
        lo = (r1 - mid.astype(F32)).astype(BF16)
        bcum = (jnp.dot(prefix, hi, preferred_element_type=F32)
                + jnp.dot(prefix, mid, preferred_element_type=F32)
                + jnp.dot(prefix, lo, preferred_element_type=F32))
        b_last = bcum[CHUNK - 1:CHUNK, :]
        q = q_ref[0, sl, :].astype(F32) * GLA_DK ** -0.5
        k = k_ref[0, sl, :].astype(F32)
        q_d = (q * jnp.exp(bcum)).astype(BF16)
        k_d = (k * jnp.exp(-bcum)).astype(BF16)
        k_l = (k * jnp.exp(b_last - bcum)).astype(BF16)
        decay = jnp.exp(b_last)
        for hd in range(GLA_HEADS):
            ks = slice(hd * GLA_DK, (hd + 1) * GLA_DK)
            vs = slice(hd * GLA_DV, (hd + 1) * GLA_DV)
            v = v_ref[0, sl, vs]
            att = lax.dot_general(q_d[:, ks], k_d[:, ks], _NT, preferred_element_type=F32)
            att = jnp.where(causal, att, 0.0).astype(BF16)
            st = state_ref[hd]
            o = (jnp.dot(att, v, preferred_element_type=F32)
                 + lax.dot_general(q_d[:, ks], st.astype(BF16), _NT, preferred_element_type=F32))
            state_ref[hd] = decay[:, ks] * st + lax.dot_general(
                v, k_l[:, ks], _TN, preferred_element_type=F32)
            og = og_ref[0, sl, vs].astype(F32)
            o_ref[0, sl, vs] = (_rms(o, gnorm) * (og * _sigmoid(og))).astype(BF16)
        return 0

    lax.fori_loop(0, s // CHUNK, chunk, 0)


def _gla(u, gate, wup, bup, gnorm):
    b, s, _ = u.shape
    hk = GLA_HEADS * GLA_DK
    return pl.pallas_call(
        _gla_body,
        grid=(b,),
        in_specs=[
            pl.BlockSpec((1, s, LANES), lambda bi: (bi, 0, 0)),
            pl.BlockSpec((LANES, hk), lambda bi: (0, 0)),
            pl.BlockSpec((1, hk), lambda bi: (0, 0)),
            pl.BlockSpec((1, GLA_DV), lambda bi: (0, 0)),
            pl.BlockSpec((1, s, hk), lambda bi: (bi, 0, U_QB // hk)),
            pl.BlockSpec((1, s, hk), lambda bi: (bi, 0, U_KB // hk)),
            pl.BlockSpec((1, s, GLA_WIDTH), lambda bi: (bi, 0, U_VB // GLA_WIDTH)),
            pl.BlockSpec((1, s, GLA_WIDTH), lambda bi: (bi, 0, U_OB // GLA_WIDTH)),
        ],
        out_specs=pl.BlockSpec((1, s, GLA_WIDTH), lambda bi: (bi, 0, 0)),
        out_shape=jax.ShapeDtypeStruct((b, s, GLA_WIDTH), BF16),
        scratch_shapes=[pltpu.VMEM((s, hk), F32), pltpu.VMEM((GLA_HEADS, GLA_DV, GLA_DK), F32)],
        compiler_params=_params("parallel"),
        name="gla",
    )(gate, wup, bup, gnorm, u, u, u, u)


def _outproj_body(h_ref, oa_ref, ob_ref, oc_ref, wa_ref, wb_ref, wc_ref, o_ref):
    o_ref[...] = (h_ref[...]
                  + jnp.dot(oa_ref[...], wa_ref[...], preferred_element_type=F32)
                  + jnp.dot(ob_ref[...], wb_ref[...], preferred_element_type=F32)
                  + jnp.dot(oc_ref[...], wc_ref[...], preferred_element_type=F32))


def _outproj(h, oa, ob, oc, wa, wb, wc, *, tm):
    m, d = h.shape

    def rows(width):
        return pl.BlockSpec((tm, width), lambda i: (i, 0))

    def whole(width):
        return pl.BlockSpec((width, d), lambda i: (0, 0))

    return pl.pallas_call(
        _outproj_body,
        grid=(m // tm,),
        in_specs=[rows(d), rows(DA_WIDTH), rows(GLA_WIDTH), rows(SB_WIDTH),
                  whole(DA_WIDTH), whole(GLA_WIDTH), whole(SB_WIDTH)],
        out_specs=rows(d),
        out_shape=jax.ShapeDtypeStruct((m, d), F32),
        compiler_params=_params("parallel"),
        name="outproj",
    )(h, oa, ob, oc, wa, wb, wc)


def _ple_body(h_ref, p_ref, g_ref, wg_ref, wp_ref, gf_ref, o_ref, *, final):
    x = h_ref[...]
    n = _rms(x, g_ref[...]).astype(BF16)
    gate = _sigmoid(jnp.dot(n, wg_ref[...], preferred_element_type=F32))
    emb = jnp.dot(p_ref[...].astype(BF16), wp_ref[...], preferred_element_type=F32)
    y = x + emb * gate
    o_ref[...] = _rms(y, gf_ref[...]) if final else y


def _ple(h, p, g, wg, wp, gf, *, tm, final):
    m, d = h.shape
    pd = p.shape[1]
    return pl.pallas_call(
        functools.partial(_ple_body, final=final),
        grid=(m // tm,),
        in_specs=[
            pl.BlockSpec((tm, d), lambda i: (i, 0)),
            pl.BlockSpec((tm, pd), lambda i: (i, 0)),
            pl.BlockSpec((1, d), lambda i: (0, 0)),
            pl.BlockSpec((d, d), lambda i: (0, 0)),
            pl.BlockSpec((pd, d), lambda i: (0, 0)),
            pl.BlockSpec((1, d), lambda i: (0, 0)),
        ],
        out_specs=pl.BlockSpec((tm, d), lambda i: (i, 0)),
        out_shape=jax.ShapeDtypeStruct((m, d), F32),
        compiler_params=_params("parallel"),
        name="ple",
    )(h, p, g, wg, wp, gf)


def _reorder_w_in(w):
    sizes = (DA_WIDTH, DA_WIDTH, DA_WIDTH, GLA_HEADS * GLA_DK, GLA_HEADS * GLA_DK, GLA_WIDTH,
             GLA_RANK, GLA_WIDTH, SB_WIDTH, SB_WIDTH, SB_WIDTH)
    parts, start = [], 0
    for size in sizes:
        parts.append(w[:, start:start + size])
        start += size
    qa, ka, va, qb, kb, vb, gb, ob, qc, kc, vc = parts
    main = jnp.concatenate([qa, ka, va, qb, vb, ob, kb, qc, kc, vc], axis=1).astype(BF16)
    gate = jnp.pad(gb, ((0, 0), (0, LANES - GLA_RANK))).astype(BF16)
    return main, gate


def kernel(x, p, positions, g_ffn1, w_ffn1_gate, w_ffn1_up, w_ffn1_down, g_mix, w_in, w_gla_gate_up, b_gla_gate, g_gla_norm, lambda_q1, lambda_k1, lambda_q2, lambda_k2, g_diff_norm, w_out, g_ffn2, w_ffn2_gate, w_ffn2_up, w_ffn2_down, g_ple, w_ple_gate, w_ple_proj, g_final):
    b, s, d = x.shape
    m = b * s
    tm = min(512, m)
    tm_big = min(1024, m)
    tq = min(256, s)
    assert m % tm_big == 0 and m % tm == 0 and s % tq == 0 and tq % CHUNK == 0

    inv_freq = ROPE_THETA ** (-jnp.arange(0, ROPE_DIMS, 2, dtype=F32) / ROPE_DIMS)
    freq_lanes = jnp.tile(inv_freq, LANES // inv_freq.shape[0]).reshape(1, LANES)
    pos_lanes = jnp.broadcast_to(positions.astype(F32).reshape(m, 1), (m, LANES))
    rope_c, rope_s1, rope_s2 = _rope_tables(pos_lanes, freq_lanes, tm=tm)

    h = x.reshape(m, d)
    for i in range(DEPTH):
        row = lambda a: a[i].reshape(1, -1)
        h = _ffn(h, row(g_ffn1), w_ffn1_gate[i].astype(BF16), w_ffn1_up[i].astype(BF16),
                 w_ffn1_down[i].astype(BF16), tm=tm_big, tf=512)

        w_main, w_gate = _reorder_w_in(w_in[i])
        u, gate_low = _inproj(h, row(g_mix), w_main, w_gate, tm=tm_big, tn=1024)
        qr, kr = _rope_apply(u, rope_c, rope_s1, rope_s2, tm=tm)
        u3 = u.reshape(b, s, U_WIDTH)

        lam_init = 0.8 - 0.6 * math.exp(-0.3 * i)
        lam_p = jnp.stack([lambda_q1[i], lambda_k1[i], lambda_q2[i], lambda_k2[i]])
        o_a = _diff_attention(qr.reshape(b, s, DA_WIDTH), kr.reshape(b, s, DA_WIDTH), u3, lam_p,
                              row(g_diff_norm), tq=tq, lam_init=lam_init)
        wup = jnp.pad(w_gla_gate_up[i], ((0, LANES - GLA_RANK), (0, 0))).astype(BF16)
        o_b = _gla(u3, gate_low.reshape(b, s, LANES), wup, row(b_gla_gate), row(g_gla_norm))
        o_c = _sb_attention(u3, tq=tq)

        wo = w_out[i].astype(BF16)
        h = _outproj(h, o_a.reshape(m, DA_WIDTH), o_b.reshape(m, GLA_WIDTH),
                     o_c.reshape(m, SB_WIDTH), wo[:DA_WIDTH], wo[DA_WIDTH:DA_WIDTH + GLA_WIDTH],
                     wo[DA_WIDTH + GLA_WIDTH:], tm=tm)

        h = _ffn(h, row(g_ffn2), w_ffn2_gate[i].astype(BF16), w_ffn2_up[i].astype(BF16),
                 w_ffn2_down[i].astype(BF16), tm=tm_big, tf=512)
        h = _ple(h, p[i].reshape(m, PLE_DIM), row(g_ple), w_ple_gate[i].astype(BF16),
                 w_ple_proj[i].astype(BF16), g_final.reshape(1, -1), tm=tm,
                 final=(i == DEPTH - 1))
    return h.reshape(b, s, d)
```

```python
import functools
import math

import jax
import jax.numpy as jnp
from jax import lax
from jax.experimental import pallas as pl
from jax.experimental.pallas import tpu as pltpu

F32 = jnp.float32
BF16 = jnp.bfloat16

D_MODEL = 2048
DEPTH = 2
CHUNK = 64
DA_HEADS = 4
DA_QK = 64
DA_V = 128
DA_WIDTH = 512
GLA_HEADS = 4
GLA_DK = 128
GLA_DV = 256
GLA_RANK = 16
GLA_GATE_NORM = 16.0
GLA_WIDTH = 1024
SB_HEADS = 8
SB_DIM = 64
SB_WIDTH = 512
ROPE_THETA = 500000.0
ROPE_DIMS = 16
D_FF = 5632
PLE_DIM = 256
EPS = 1e-6

LANES = 128
U_QA, U_KA, U_VA, U_QB, U_VB, U_OB, U_KB, U_QC, U_KC, U_VC = (
    0, 512, 1024, 1536, 2048, 3072, 4096, 4608, 5120, 5632)
U_WIDTH = 6144

VMEM_LIMIT = 56 * 1024 * 1024
GLA_CHUNKS_PER_TRIP = 4

_NT = (((1,), (1,)), ((), ()))
_TN = (((0,), (0,)), ((), ()))


def _params(*sem):
    return pltpu.CompilerParams(dimension_semantics=sem, vmem_limit_bytes=VMEM_LIMIT)


def _rms(x, g):
    return x * lax.rsqrt(jnp.mean(x * x, axis=-1, keepdims=True) + EPS) * g


def _log_sigmoid(z):
    return jnp.minimum(z, 0.0) - jnp.log(1.0 + jnp.exp(-jnp.abs(z)))


def _sigmoid(z):
    return 1.0 / (1.0 + jnp.exp(-z))


def _ffn_body(h_ref, g_ref, wg_ref, wu_ref, wd_ref, o_ref, n_ref):
    @pl.when(pl.program_id(1) == 0)
    def _():
        x = h_ref[...]
        n_ref[...] = _rms(x, g_ref[...]).astype(BF16)
        o_ref[...] = x

    n = n_ref[...]
    a = jnp.dot(n, wg_ref[...], preferred_element_type=F32)
    b = jnp.dot(n, wu_ref[...], preferred_element_type=F32)
    act = (0.5 * a * _sigmoid(a) * b).astype(BF16)
    o_ref[...] += jnp.dot(act, wd_ref[...], preferred_element_type=F32)


def _ffn(h, g, wg, wu, wd, *, tm, tf):
    m, d = h.shape
    f = wg.shape[1]
    return pl.pallas_call(
        _ffn_body,
        grid=(m // tm, f // tf),
        in_specs=[
            pl.BlockSpec((tm, d), lambda i, j: (i, 0)),
            pl.BlockSpec((1, d), lambda i, j: (0, 0)),
            pl.BlockSpec((d, tf), lambda i, j: (0, j)),
            pl.BlockSpec((d, tf), lambda i, j: (0, j)),
            pl.BlockSpec((tf, d), lambda i, j: (j, 0)),
        ],
        out_specs=pl.BlockSpec((tm, d), lambda i, j: (i, 0)),
        out_shape=jax.ShapeDtypeStruct((m, d), F32),
        scratch_shapes=[pltpu.VMEM((tm, d), BF16)],
        compiler_params=_params("parallel", "arbitrary"),
        name="ffn",
    )(h, g, wg, wu, wd)


def _inproj_body(h_ref, g_ref, w_ref, wgate_ref, u_ref, gate_ref, n_ref):
    @pl.when(pl.program_id(1) == 0)
    def _():
        n = _rms(h_ref[...], g_ref[...]).astype(BF16)
        n_ref[...] = n
        gate_ref[...] = jnp.dot(n, wgate_ref[...], preferred_element_type=F32)

    u_ref[...] = jnp.dot(n_ref[...], w_ref[...], preferred_element_type=F32).astype(BF16)


def _inproj(h, g, w, wgate, *, tm, tn):
    m, d = h.shape
    n = w.shape[1]
    return pl.pallas_call(
        _inproj_body,
        grid=(m // tm, n // tn),
        in_specs=[
            pl.BlockSpec((tm, d), lambda i, j: (i, 0)),
            pl.BlockSpec((1, d), lambda i, j: (0, 0)),
            pl.BlockSpec((d, tn), lambda i, j: (0, j)),
            pl.BlockSpec((d, LANES), lambda i, j: (0, 0)),
        ],
        out_specs=[
            pl.BlockSpec((tm, tn), lambda i, j: (i, j)),
            pl.BlockSpec((tm, LANES), lambda i, j: (i, 0)),
        ],
        out_shape=[
            jax.ShapeDtypeStruct((m, n), BF16),
            jax.ShapeDtypeStruct((m, LANES), F32),
        ],
        scratch_shapes=[pltpu.VMEM((tm, d), BF16)],
        compiler_params=_params("parallel", "arbitrary"),
        name="inproj",
    )(h, g, w, wgate)


def _rope_table_body(pos_ref, freq_ref, c_ref, s1_ref, s2_ref):
    ang = pos_ref[...] * freq_ref[...]
    c = jnp.cos(ang)
    s = jnp.sin(ang)
    dim = lax.broadcasted_iota(jnp.int32, ang.shape, 1) % DA_QK
    half = ROPE_DIMS // 2
    c_ref[...] = jnp.where(dim < ROPE_DIMS, c, 1.0)
    s1_ref[...] = jnp.where(dim < half, -s, 0.0)
    s2_ref[...] = jnp.where((dim >= half) & (dim < ROPE_DIMS), s, 0.0)


def _rope_tables(pos_lanes, freq_lanes, *, tm):
    m = pos_lanes.shape[0]
    spec = pl.BlockSpec((tm, LANES), lambda i: (i, 0))
    return pl.pallas_call(
        _rope_table_body,
        grid=(m // tm,),
        in_specs=[spec, pl.BlockSpec((1, LANES), lambda i: (0, 0))],
        out_specs=[spec, spec, spec],
        out_shape=[jax.ShapeDtypeStruct((m, LANES), F32)] * 3,
        compiler_params=_params("parallel"),
        name="rope_tables",
    )(pos_lanes, freq_lanes)


def _split_lane_halves(x):
    lane = lax.broadcasted_iota(jnp.int32, x.shape, 1)
    zero = jnp.zeros_like(x)
    return jnp.concatenate(
        [jnp.where(lane < LANES // 2, x, zero), jnp.where(lane >= LANES // 2, x, zero)], axis=0)


def _fold_lanes(x, op):
    out = x[:, :LANES]
    for g in range(1, x.shape[1] // LANES):
        out = op(out, x[:, g * LANES:(g + 1) * LANES])
    return out


def _diff_body(lam_ref, gsub_ref, cos_ref, sin_lo_ref, sin_hi_ref, q_ref, k_ref, v_ref, o_ref,
               qr_ref, kr_ref, s_ref, p_ref, *, tq, lam_init):
    nq = q_ref.shape[1] // tq
    half = ROPE_DIMS // 2
    q_scale = DA_QK ** -0.5 * math.log2(math.e)
    cos, sin_lo, sin_hi = cos_ref[0], sin_lo_ref[0], sin_hi_ref[0]

    def rot(x):
        return (x * cos + pltpu.roll(x, LANES - half, 1) * sin_lo
                + pltpu.roll(x, half, 1) * sin_hi)

    qr_ref[...] = (rot(q_ref[0].astype(F32)) * q_scale).astype(BF16)
    kr_ref[...] = rot(k_ref[0].astype(F32)).astype(BF16)

    lam_p = lam_ref[...]
    lam = (jnp.exp(jnp.sum(lam_p[0:1] * lam_p[1:2], axis=-1, keepdims=True))
           - jnp.exp(jnp.sum(lam_p[2:3] * lam_p[3:4], axis=-1, keepdims=True)) + lam_init)
    gsub = gsub_ref[...]
    row = lax.broadcasted_iota(jnp.int32, (2 * tq, tq), 0) % tq
    col = lax.broadcasted_iota(jnp.int32, (2 * tq, tq), 1)
    chunk_mask = col // CHUNK <= row // CHUNK

    for c in range(nq):
        buf = c % 2
        qs = _split_lane_halves(qr_ref[c * tq:(c + 1) * tq, :])
        mx = None
        for j in range(c + 1):
            ks = slice(j * tq, (j + 1) * tq)
            s = lax.dot_general(qs, kr_ref[ks, :], _NT, preferred_element_type=F32)
            if j == c:
                s = jnp.where(chunk_mask, s, -jnp.inf)
            s_ref[buf, :, ks] = s
            mj = _fold_lanes(s, jnp.maximum)
            mx = mj if mx is None else jnp.maximum(mx, mj)
        m = jnp.max(mx, axis=-1, keepdims=True)
        ls = None
        for j in range(c + 1):
            ks = slice(j * tq, (j + 1) * tq)
            p = jnp.exp2(s_ref[buf, :, ks] - m)
            p_ref[buf, :, ks] = p.astype(BF16)
            pj = _fold_lanes(p, jnp.add)
            ls = pj if ls is None else ls + pj
        l = jnp.sum(ls, axis=-1, keepdims=True)
        n = (c + 1) * tq
        acc = jnp.dot(p_ref[buf, :, :n], v_ref[0, :n, :], preferred_element_type=F32)
        o = acc / l
        o = o[:tq] - lam * o[tq:]
        o_ref[0, c * tq:(c + 1) * tq, :] = (_rms(o, gsub) * (1.0 - lam_init)).astype(BF16)


def _diff_attention(u, rope, lam_p, gsub, *, tq, lam_init):
    b, s, _ = u.shape
    table = pl.BlockSpec((1, s, LANES), lambda bi, hi: (bi, 0, 0))
    return pl.pallas_call(
        functools.partial(_diff_body, tq=tq, lam_init=lam_init),
        grid=(b, DA_HEADS),
        in_specs=[
            pl.BlockSpec((4, DA_QK), lambda bi, hi: (0, 0)),
            pl.BlockSpec((1, DA_V), lambda bi, hi: (0, 0)),
            table, table, table,
            pl.BlockSpec((1, s, LANES), lambda bi, hi: (bi, 0, U_QA // LANES + hi)),
            pl.BlockSpec((1, s, LANES), lambda bi, hi: (bi, 0, U_KA // LANES + hi)),
            pl.BlockSpec((1, s, LANES), lambda bi, hi: (bi, 0, U_VA // LANES + hi)),
        ],
        out_specs=pl.BlockSpec((1, s, LANES), lambda bi, hi: (bi, 0, hi)),
        out_shape=jax.ShapeDtypeStruct((b, s, DA_WIDTH), BF16),
        scratch_shapes=[pltpu.VMEM((s, LANES), BF16), pltpu.VMEM((s, LANES), BF16),
                        pltpu.VMEM((2, 2 * tq, s), F32), pltpu.VMEM((2, 2 * tq, s), BF16)],
        compiler_params=_params("parallel", "parallel"),
        name="diff_attention",
    )(lam_p, gsub, *rope, u, u, u)


def _sb_body(q_ref, k_ref, v_ref, o_ref, a_ref, *, tq):
    nq = q_ref.shape[1] // tq
    kk = lax.broadcasted_iota(jnp.int32, (2 * tq, tq), 0) % tq
    jj = lax.broadcasted_iota(jnp.int32, (2 * tq, tq), 1)
    later = jnp.where(kk > jj, 1.0, 0.0).astype(BF16)
    strictly_before = jj < kk
    lane = lax.broadcasted_iota(jnp.int32, (tq, LANES), 1)
    log2e = math.log2(math.e)

    for c in range(nq):
        buf = c % 2
        q = q_ref[0, c * tq:(c + 1) * tq, :].astype(F32) * (SB_DIM ** -0.5 * log2e)
        qs = _split_lane_halves(q.astype(BF16))
        carry = None
        for j in range(c, -1, -1):
            ks = slice(j * tq, (j + 1) * tq)
            z = lax.dot_general(qs, k_ref[0, ks, :], _NT, preferred_element_type=F32)
            softplus = jnp.log(1.0 + jnp.exp2(-jnp.abs(z))) * log2e
            log_beta = jnp.minimum(z, 0.0) - softplus
            log_1m = log_beta - z
            if j == c:
                log_1m = jnp.where(strictly_before, log_1m, 0.0)
            hi = log_1m.astype(BF16)
            lo = (log_1m - hi.astype(F32)).astype(BF16)
            tail = jnp.dot(jnp.concatenate([hi, lo], axis=1), later,
                           preferred_element_type=F32)
            expo = log_beta + tail
            if carry is not None:
                expo = expo + carry
            a = jnp.exp2(expo)
            if j == c:
                a = jnp.where(strictly_before, a, 0.0)
            a_ref[buf, :, ks] = a.astype(BF16)
            rs = jnp.sum(_fold_lanes(log_1m, jnp.add), axis=-1, keepdims=True)
            carry = rs if carry is None else carry + rs
        n = (c + 1) * tq
        acc = jnp.dot(a_ref[buf, :, :n], v_ref[0, :n, :], preferred_element_type=F32)
        o_ref[0, c * tq:(c + 1) * tq, :] = jnp.where(
            lane < LANES // 2, acc[:tq], acc[tq:]).astype(BF16)


def _sb_attention(u, *, tq):
    b, s, _ = u.shape
    return pl.pallas_call(
        functools.partial(_sb_body, tq=tq),
        grid=(b, SB_WIDTH // LANES),
        in_specs=[
            pl.BlockSpec((1, s, LANES), lambda bi, hi: (bi, 0, U_QC // LANES + hi)),
            pl.BlockSpec((1, s, LANES), lambda bi, hi: (bi, 0, U_KC // LANES + hi)),
            pl.BlockSpec((1, s, LANES), lambda bi, hi: (bi, 0, U_VC // LANES + hi)),
        ],
        out_specs=pl.BlockSpec((1, s, LANES), lambda bi, hi: (bi, 0, hi)),
        out_shape=jax.ShapeDtypeStruct((b, s, SB_WIDTH), BF16),
        scratch_shapes=[pltpu.VMEM((2, 2 * tq, s), BF16)],
        compiler_params=_params("parallel", "parallel"),
        name="sb_attention",
    )(u, u, u)


def _gla_body(gate_ref, wup_ref, bup_ref, gnorm_ref, q_ref, k_ref, v_ref, og_ref, o_ref,
              loga_ref, state_ref):
    s = q_ref.shape[1]
    pre = jnp.dot(gate_ref[0].astype(BF16), wup_ref[...], preferred_element_type=F32)
    loga_ref[...] = _log_sigmoid(pre + bup_ref[...]) * (1.0 / GLA_GATE_NORM)
    state_ref[...] = jnp.zeros_like(state_ref)

    rr = lax.broadcasted_iota(jnp.int32, (CHUNK, CHUNK), 0)
    cc = lax.broadcasted_iota(jnp.int32, (CHUNK, CHUNK), 1)
    causal = rr >= cc
    prefix = jnp.where(causal, 1.0, 0.0).astype(BF16)
    gnorm = gnorm_ref[...]

    def chunk(n):
        sl = pl.ds(pl.multiple_of(n * CHUNK, CHUNK), CHUNK)
        la = loga_ref[sl, :]
        hi = la.astype(BF16)
        r1 = la - hi.astype(F32)
        mid = r1.astype(BF16)
        lo = (r1 - mid.astype(F32)).astype(BF16)
        bcum = (jnp.dot(prefix, hi, preferred_element_type=F32)
                + jnp.dot(prefix, mid, preferred_element_type=F32)
                + jnp.dot(prefix, lo, preferred_element_type=F32))
        b_last = bcum[CHUNK - 1:CHUNK, :]
        q = q_ref[0, sl, :].astype(F32) * GLA_DK ** -0.5
        k = k_ref[0, sl, :].astype(F32)
        q_d = (q * jnp.exp(bcum)).astype(BF16)
        k_d = (k * jnp.exp(-bcum)).astype(BF16)
        k_l = (k * jnp.exp(b_last - bcum)).astype(BF16)
        decay = jnp.exp(b_last)
        for hd in range(GLA_HEADS):
            ks = slice(hd * GLA_DK, (hd + 1) * GLA_DK)
            vs = slice(hd * GLA_DV, (hd + 1) * GLA_DV)
            v = v_ref[0, sl, vs]
            att = lax.dot_general(q_d[:, ks], k_d[:, ks], _NT, preferred_element_type=F32)
            att = jnp.where(causal, att, 0.0).astype(BF16)
            st = state_ref[hd]
            o = (jnp.dot(att, v, preferred_element_type=F32)
                 + lax.dot_general(q_d[:, ks], st.astype(BF16), _NT, preferred_element_type=F32))
            state_ref[hd] = decay[:, ks] * st + lax.dot_general(
                v, k_l[:, ks], _TN, preferred_element_type=F32)
            og = og_ref[0, sl, vs].astype(F32)
            o_ref[0, sl, vs] = (_rms(o, gnorm) * (og * _sigmoid(og))).astype(BF16)

    group = math.gcd(GLA_CHUNKS_PER_TRIP, s // CHUNK)

    def trip(g, _):
        for t in range(group):
            chunk(g * group + t)
        return 0

    lax.fori_loop(0, s // CHUNK // group, trip, 0)


def _gla(u, gate, wup, bup, gnorm):
    b, s, _ = u.shape
    hk = GLA_HEADS * GLA_DK
    return pl.pallas_call(
        _gla_body,
        grid=(b,),
        in_specs=[
            pl.BlockSpec((1, s, LANES), lambda bi: (bi, 0, 0)),
            pl.BlockSpec((LANES, hk), lambda bi: (0, 0)),
            pl.BlockSpec((1, hk), lambda bi: (0, 0)),
            pl.BlockSpec((1, GLA_DV), lambda bi: (0, 0)),
            pl.BlockSpec((1, s, hk), lambda bi: (bi, 0, U_QB // hk)),
            pl.BlockSpec((1, s, hk), lambda bi: (bi, 0, U_KB // hk)),
            pl.BlockSpec((1, s, GLA_WIDTH), lambda bi: (bi, 0, U_VB // GLA_WIDTH)),
            pl.BlockSpec((1, s, GLA_WIDTH), lambda bi: (bi, 0, U_OB // GLA_WIDTH)),
        ],
        out_specs=pl.BlockSpec((1, s, GLA_WIDTH), lambda bi: (bi, 0, 0)),
        out_shape=jax.ShapeDtypeStruct((b, s, GLA_WIDTH), BF16),
        scratch_shapes=[pltpu.VMEM((s, hk), F32), pltpu.VMEM((GLA_HEADS, GLA_DV, GLA_DK), F32)],
        compiler_params=_params("parallel"),
        name="gla",
    )(gate, wup, bup, gnorm, u, u, u, u)


def _outproj_body(h_ref, oa_ref, ob_ref, oc_ref, w_ref, o_ref):
    b0, c0 = DA_WIDTH, DA_WIDTH + GLA_WIDTH
    o_ref[...] = (h_ref[...]
                  + jnp.dot(oa_ref[...], w_ref[:b0, :], preferred_element_type=F32)
                  + jnp.dot(ob_ref[...], w_ref[b0:c0, :], preferred_element_type=F32)
                  + jnp.dot(oc_ref[...], w_ref[c0:, :], preferred_element_type=F32))


def _outproj(h, oa, ob, oc, w, *, tm):
    m, d = h.shape

    def rows(width):
        return pl.BlockSpec((tm, width), lambda i: (i, 0))

    return pl.pallas_call(
        _outproj_body,
        grid=(m // tm,),
        in_specs=[rows(d), rows(DA_WIDTH), rows(GLA_WIDTH), rows(SB_WIDTH),
                  pl.BlockSpec(w.shape, lambda i: (0, 0))],
        out_specs=rows(d),
        out_shape=jax.ShapeDtypeStruct((m, d), F32),
        compiler_params=_params("parallel"),
        name="outproj",
    )(h, oa, ob, oc, w)


def _to_bf16_body(w_ref, o_ref):
    o_ref[...] = w_ref[0].astype(BF16)


def _layer_to_bf16(w, layer, *, br):
    _, r, c = w.shape
    assert r % br == 0
    return pl.pallas_call(
        _to_bf16_body,
        grid=(r // br,),
        in_specs=[pl.BlockSpec((1, br, c), lambda i: (layer, i, 0))],
        out_specs=pl.BlockSpec((br, c), lambda i: (i, 0)),
        out_shape=jax.ShapeDtypeStruct((r, c), BF16),
        compiler_params=_params("parallel"),
        name="to_bf16",
    )(w)


def _ple_body(h_ref, p_ref, g_ref, wg_ref, wp_ref, gf_ref, o_ref, *, final):
    x = h_ref[...]
    n = _rms(x, g_ref[...]).astype(BF16)
    gate = _sigmoid(jnp.dot(n, wg_ref[...], preferred_element_type=F32))
    emb = jnp.dot(p_ref[...].astype(BF16), wp_ref[...], preferred_element_type=F32)
    y = x + emb * gate
    o_ref[...] = _rms(y, gf_ref[...]) if final else y


def _ple(h, p, g, wg, wp, gf, *, tm, final):
    m, d = h.shape
    pd = p.shape[1]
    return pl.pallas_call(
        functools.partial(_ple_body, final=final),
        grid=(m // tm,),
        in_specs=[
            pl.BlockSpec((tm, d), lambda i: (i, 0)),
            pl.BlockSpec((tm, pd), lambda i: (i, 0)),
            pl.BlockSpec((1, d), lambda i: (0, 0)),
            pl.BlockSpec((d, d), lambda i: (0, 0)),
            pl.BlockSpec((pd, d), lambda i: (0, 0)),
            pl.BlockSpec((1, d), lambda i: (0, 0)),
        ],
        out_specs=pl.BlockSpec((tm, d), lambda i: (i, 0)),
        out_shape=jax.ShapeDtypeStruct((m, d), F32),
        compiler_params=_params("parallel"),
        name="ple",
    )(h, p, g, wg, wp, gf)


def _reorder_w_in(w):
    sizes = (DA_WIDTH, DA_WIDTH, DA_WIDTH, GLA_HEADS * GLA_DK, GLA_HEADS * GLA_DK, GLA_WIDTH,
             GLA_RANK, GLA_WIDTH, SB_WIDTH, SB_WIDTH, SB_WIDTH)
    parts, start = [], 0
    for size in sizes:
        parts.append(w[:, start:start + size])
        start += size
    qa, ka, va, qb, kb, vb, gb, ob, qc, kc, vc = parts
    main = jnp.concatenate([qa, ka, va, qb, vb, ob, kb, qc, kc, vc], axis=1).astype(BF16)
    gate = jnp.pad(gb, ((0, 0), (0, LANES - GLA_RANK))).astype(BF16)
    return main, gate


def kernel(x, p, positions, g_ffn1, w_ffn1_gate, w_ffn1_up, w_ffn1_down, g_mix, w_in, w_gla_gate_up, b_gla_gate, g_gla_norm, lambda_q1, lambda_k1, lambda_q2, lambda_k2, g_diff_norm, w_out, g_ffn2, w_ffn2_gate, w_ffn2_up, w_ffn2_down, g_ple, w_ple_gate, w_ple_proj, g_final):
    b, s, d = x.shape
    m = b * s
    tm = min(512, m)
    tm_big = min(1024, m)
    tq = min(256, s)
    assert m % tm_big == 0 and m % tm == 0 and s % tq == 0 and tq % CHUNK == 0

    inv_freq = ROPE_THETA ** (-jnp.arange(0, ROPE_DIMS, 2, dtype=F32) / ROPE_DIMS)
    freq_lanes = jnp.tile(inv_freq, LANES // inv_freq.shape[0]).reshape(1, LANES)
    pos_lanes = jnp.broadcast_to(positions.astype(F32).reshape(m, 1), (m, LANES))
    rope_c, rope_s1, rope_s2 = _rope_tables(pos_lanes, freq_lanes, tm=tm)

    rope = [t.reshape(b, s, LANES) for t in (rope_c, rope_s1, rope_s2)]

    def ffn_weights(wg, wu, wd, i):
        return (_layer_to_bf16(wg, i, br=256), _layer_to_bf16(wu, i, br=256),
                _layer_to_bf16(wd, i, br=512))

    h = x.reshape(m, d)
    for i in range(DEPTH):
        row = lambda a: a[i].reshape(1, -1)
        h = _ffn(h, row(g_ffn1), *ffn_weights(w_ffn1_gate, w_ffn1_up, w_ffn1_down, i),
                 tm=tm_big, tf=512)

        w_main, w_gate = _reorder_w_in(w_in[i])
        u, gate_low = _inproj(h, row(g_mix), w_main, w_gate, tm=tm_big, tn=1024)
        u3 = u.reshape(b, s, U_WIDTH)

        lam_init = 0.8 - 0.6 * math.exp(-0.3 * i)
        lam_p = jnp.stack([lambda_q1[i], lambda_k1[i], lambda_q2[i], lambda_k2[i]])
        o_a = _diff_attention(u3, rope, lam_p, row(g_diff_norm), tq=tq, lam_init=lam_init)
        wup = jnp.pad(w_gla_gate_up[i], ((0, LANES - GLA_RANK), (0, 0))).astype(BF16)
        o_b = _gla(u3, gate_low.reshape(b, s, LANES), wup, row(b_gla_gate), row(g_gla_norm))
        o_c = _sb_attention(u3, tq=tq)

        h = _outproj(h, o_a.reshape(m, DA_WIDTH), o_b.reshape(m, GLA_WIDTH),
                     o_c.reshape(m, SB_WIDTH), _layer_to_bf16(w_out, i, br=512), tm=tm)

        h = _ffn(h, row(g_ffn2), *ffn_weights(w_ffn2_gate, w_ffn2_up, w_ffn2_down, i),
                 tm=tm_big, tf=512)
        h = _ple(h, p[i].reshape(m, PLE_DIM), row(g_ple), _layer_to_bf16(w_ple_gate, i, br=512),
                 w_ple_proj[i].astype(BF16), g_final.reshape(1, -1), tm=tm,
                 final=(i == DEPTH - 1))
    return h.reshape(b, s, d)
```

```python
import functools
import math

import jax
import jax.numpy as jnp
from jax import lax
from jax.experimental import pallas as pl
from jax.experimental.pallas import tpu as pltpu

F32 = jnp.float32
BF16 = jnp.bfloat16

D_MODEL = 2048
DEPTH = 2
CHUNK = 64
DA_HEADS = 4
DA_QK = 64
DA_V = 128
DA_WIDTH = 512
GLA_HEADS = 4
GLA_DK = 128
GLA_DV = 256
GLA_RANK = 16
GLA_GATE_NORM = 16.0
GLA_WIDTH = 1024
SB_HEADS = 8
SB_DIM = 64
SB_WIDTH = 512
ROPE_THETA = 500000.0
ROPE_DIMS = 16
D_FF = 5632
PLE_DIM = 256
EPS = 1e-6

LANES = 128
U_QA, U_KA, U_VA, U_QB, U_VB, U_OB, U_KB, U_QC, U_KC, U_VC = (
    0, 512, 1024, 1536, 2048, 3072, 4096, 4608, 5120, 5632)
U_WIDTH = 6144

VMEM_LIMIT = 56 * 1024 * 1024
GLA_CHUNKS_PER_TRIP = 4

_NT = (((1,), (1,)), ((), ()))
_TN = (((0,), (0,)), ((), ()))


def _params(*sem):
    return pltpu.CompilerParams(dimension_semantics=sem, vmem_limit_bytes=VMEM_LIMIT)


def _rms(x, g):
    return x * lax.rsqrt(jnp.mean(x * x, axis=-1, keepdims=True) + EPS) * g


def _log_sigmoid(z):
    return jnp.minimum(z, 0.0) - jnp.log(1.0 + jnp.exp(-jnp.abs(z)))


def _sigmoid(z):
    return 1.0 / (1.0 + jnp.exp(-z))


def _ffn_body(h_ref, g_ref, wg_ref, wu_ref, wd_ref, o_ref, n_ref):
    @pl.when(pl.program_id(1) == 0)
    def _():
        x = h_ref[...]
        n_ref[...] = _rms(x, g_ref[...]).astype(BF16)
        o_ref[...] = x

    n = n_ref[...]
    a = jnp.dot(n, wg_ref[...], preferred_element_type=F32)
    b = jnp.dot(n, wu_ref[...], preferred_element_type=F32)
    act = (0.5 * a * _sigmoid(a) * b).astype(BF16)
    o_ref[...] += jnp.dot(act, wd_ref[...], preferred_element_type=F32)


def _ffn(h, g, wg, wu, wd, *, tm, tf):
    m, d = h.shape
    f = wg.shape[1]
    return pl.pallas_call(
        _ffn_body,
        grid=(m // tm, f // tf),
        in_specs=[
            pl.BlockSpec((tm, d), lambda i, j: (i, 0)),
            pl.BlockSpec((1, d), lambda i, j: (0, 0)),
            pl.BlockSpec((d, tf), lambda i, j: (0, j)),
            pl.BlockSpec((d, tf), lambda i, j: (0, j)),
            pl.BlockSpec((tf, d), lambda i, j: (j, 0)),
        ],
        out_specs=pl.BlockSpec((tm, d), lambda i, j: (i, 0)),
        out_shape=jax.ShapeDtypeStruct((m, d), F32),
        scratch_shapes=[pltpu.VMEM((tm, d), BF16)],
        compiler_params=_params("parallel", "arbitrary"),
        name="ffn",
    )(h, g, wg, wu, wd)


def _inproj_body(h_ref, g_ref, w_ref, wgate_ref, u_ref, gate_ref, n_ref):
    @pl.when(pl.program_id(1) == 0)
    def _():
        n = _rms(h_ref[...], g_ref[...]).astype(BF16)
        n_ref[...] = n
        gate_ref[...] = jnp.dot(n, wgate_ref[...], preferred_element_type=F32)

    u_ref[...] = jnp.dot(n_ref[...], w_ref[...], preferred_element_type=F32).astype(BF16)


def _inproj(h, g, w, wgate, *, tm, tn):
    m, d = h.shape
    n = w.shape[1]
    return pl.pallas_call(
        _inproj_body,
        grid=(m // tm, n // tn),
        in_specs=[
            pl.BlockSpec((tm, d), lambda i, j: (i, 0)),
            pl.BlockSpec((1, d), lambda i, j: (0, 0)),
            pl.BlockSpec((d, tn), lambda i, j: (0, j)),
            pl.BlockSpec((d, LANES), lambda i, j: (0, 0)),
        ],
        out_specs=[
            pl.BlockSpec((tm, tn), lambda i, j: (i, j)),
            pl.BlockSpec((tm, LANES), lambda i, j: (i, 0)),
        ],
        out_shape=[
            jax.ShapeDtypeStruct((m, n), BF16),
            jax.ShapeDtypeStruct((m, LANES), F32),
        ],
        scratch_shapes=[pltpu.VMEM((tm, d), BF16)],
        compiler_params=_params("parallel", "arbitrary"),
        name="inproj",
    )(h, g, w, wgate)


def _rope_table_body(pos_ref, freq_ref, c_ref, s1_ref, s2_ref):
    ang = pos_ref[...] * freq_ref[...]
    c = jnp.cos(ang)
    s = jnp.sin(ang)
    dim = lax.broadcasted_iota(jnp.int32, ang.shape, 1) % DA_QK
    half = ROPE_DIMS // 2
    c_ref[...] = jnp.where(dim < ROPE_DIMS, c, 1.0)
    s1_ref[...] = jnp.where(dim < half, -s, 0.0)
    s2_ref[...] = jnp.where((dim >= half) & (dim < ROPE_DIMS), s, 0.0)


def _rope_tables(pos_lanes, freq_lanes, *, tm):
    m = pos_lanes.shape[0]
    spec = pl.BlockSpec((tm, LANES), lambda i: (i, 0))
    return pl.pallas_call(
        _rope_table_body,
        grid=(m // tm,),
        in_specs=[spec, pl.BlockSpec((1, LANES), lambda i: (0, 0))],
        out_specs=[spec, spec, spec],
        out_shape=[jax.ShapeDtypeStruct((m, LANES), F32)] * 3,
        compiler_params=_params("parallel"),
        name="rope_tables",
    )(pos_lanes, freq_lanes)


def _split_lane_halves(x):
    lane = lax.broadcasted_iota(jnp.int32, x.shape, 1)
    zero = jnp.zeros_like(x)
    return jnp.concatenate(
        [jnp.where(lane < LANES // 2, x, zero), jnp.where(lane >= LANES // 2, x, zero)], axis=0)


def _fold_lanes(x, op):
    out = x[:, :LANES]
    for g in range(1, x.shape[1] // LANES):
        out = op(out, x[:, g * LANES:(g + 1) * LANES])
    return out


def _mixers_body(lam_ref, gsub_ref, cos_ref, sin_lo_ref, sin_hi_ref, qa_ref, ka_ref, va_ref,
                 qc_ref, kc_ref, vc_ref, oa_ref, oc_ref, qr_ref, kr_ref, s_ref, p_ref, a_ref,
                 *, tq, lam_init):
    nq = qa_ref.shape[1] // tq
    log2e = math.log2(math.e)
    row = lax.broadcasted_iota(jnp.int32, (2 * tq, tq), 0) % tq
    col = lax.broadcasted_iota(jnp.int32, (2 * tq, tq), 1)
    lane = lax.broadcasted_iota(jnp.int32, (tq, LANES), 1)

    half = ROPE_DIMS // 2
    cos, sin_lo, sin_hi = cos_ref[0], sin_lo_ref[0], sin_hi_ref[0]

    def rot(x):
        return (x * cos + pltpu.roll(x, LANES - half, 1) * sin_lo
                + pltpu.roll(x, half, 1) * sin_hi)

    qr_ref[...] = (rot(qa_ref[0].astype(F32)) * (DA_QK ** -0.5 * log2e)).astype(BF16)
    kr_ref[...] = rot(ka_ref[0].astype(F32)).astype(BF16)
    lam_p = lam_ref[...]
    lam = (jnp.exp(jnp.sum(lam_p[0:1] * lam_p[1:2], axis=-1, keepdims=True))
           - jnp.exp(jnp.sum(lam_p[2:3] * lam_p[3:4], axis=-1, keepdims=True)) + lam_init)
    gsub = gsub_ref[...]
    chunk_mask = col // CHUNK <= row // CHUNK

    def diff_block(c):
        buf = c % 2
        qs = _split_lane_halves(qr_ref[c * tq:(c + 1) * tq, :])
        mx = None
        for j in range(c + 1):
            ks = slice(j * tq, (j + 1) * tq)
            s = lax.dot_general(qs, kr_ref[ks, :], _NT, preferred_element_type=F32)
            if j == c:
                s = jnp.where(chunk_mask, s, -jnp.inf)
            s_ref[buf, :, ks] = s
            mj = _fold_lanes(s, jnp.maximum)
            mx = mj if mx is None else jnp.maximum(mx, mj)
            yield
        m = jnp.max(mx, axis=-1, keepdims=True)
        ls = None
        for j in range(c + 1):
            ks = slice(j * tq, (j + 1) * tq)
            p = jnp.exp2(s_ref[buf, :, ks] - m)
            p_ref[buf, :, ks] = p.astype(BF16)
            pj = _fold_lanes(p, jnp.add)
            ls = pj if ls is None else ls + pj
            yield
        l = jnp.sum(ls, axis=-1, keepdims=True)
        n = (c + 1) * tq
        acc = jnp.dot(p_ref[buf, :, :n], va_ref[0, :n, :], preferred_element_type=F32)
        o = acc / l
        o = o[:tq] - lam * o[tq:]
        oa_ref[0, c * tq:(c + 1) * tq, :] = (_rms(o, gsub) * (1.0 - lam_init)).astype(BF16)

    later = jnp.where(row > col, 1.0, 0.0).astype(BF16)
    strictly_before = col < row

    def sb_block(c):
        buf = c % 2
        q = qc_ref[0, c * tq:(c + 1) * tq, :].astype(F32) * (SB_DIM ** -0.5 * log2e)
        qs = _split_lane_halves(q.astype(BF16))
        carry = None
        for j in range(c, -1, -1):
            ks = slice(j * tq, (j + 1) * tq)
            z = lax.dot_general(qs, kc_ref[0, ks, :], _NT, preferred_element_type=F32)
            softplus = jnp.log(1.0 + jnp.exp2(-jnp.abs(z))) * log2e
            log_beta = jnp.minimum(z, 0.0) - softplus
            log_1m = log_beta - z
            if j == c:
                log_1m = jnp.where(strictly_before, log_1m, 0.0)
            hi = log_1m.astype(BF16)
            lo = (log_1m - hi.astype(F32)).astype(BF16)
            tail = jnp.dot(jnp.concatenate([hi, lo], axis=1), later,
                           preferred_element_type=F32)
            expo = log_beta + tail
            if carry is not None:
                expo = expo + carry
            a = jnp.exp2(expo)
            if j == c:
                a = jnp.where(strictly_before, a, 0.0)
            a_ref[buf, :, ks] = a.astype(BF16)
            rs = jnp.sum(_fold_lanes(log_1m, jnp.add), axis=-1, keepdims=True)
            carry = rs if carry is None else carry + rs
            yield
        n = (c + 1) * tq
        acc = jnp.dot(a_ref[buf, :, :n], vc_ref[0, :n, :], preferred_element_type=F32)
        oc_ref[0, c * tq:(c + 1) * tq, :] = jnp.where(
            lane < LANES // 2, acc[:tq], acc[tq:]).astype(BF16)

    for c in range(nq):
        sweeps = [diff_block(c), sb_block(c)]
        while sweeps:
            for sweep in list(sweeps):
                if next(sweep, "done") == "done":
                    sweeps.remove(sweep)


def _mixers(u, rope, lam_p, gsub, *, tq, lam_init):
    b, s, _ = u.shape
    assert DA_HEADS == SB_WIDTH // LANES

    def slab(offset):
        return pl.BlockSpec((1, s, LANES), lambda bi, hi: (bi, 0, offset // LANES + hi))

    table = pl.BlockSpec((1, s, LANES), lambda bi, hi: (bi, 0, 0))
    return pl.pallas_call(
        functools.partial(_mixers_body, tq=tq, lam_init=lam_init),
        grid=(b, DA_HEADS),
        in_specs=[
            pl.BlockSpec((4, DA_QK), lambda bi, hi: (0, 0)),
            pl.BlockSpec((1, DA_V), lambda bi, hi: (0, 0)),
            table, table, table,
            slab(U_QA), slab(U_KA), slab(U_VA), slab(U_QC), slab(U_KC), slab(U_VC),
        ],
        out_specs=[slab(0), slab(0)],
        out_shape=[jax.ShapeDtypeStruct((b, s, DA_WIDTH), BF16),
                   jax.ShapeDtypeStruct((b, s, SB_WIDTH), BF16)],
        scratch_shapes=[pltpu.VMEM((s, LANES), BF16), pltpu.VMEM((s, LANES), BF16),
                        pltpu.VMEM((2, 2 * tq, s), F32), pltpu.VMEM((2, 2 * tq, s), BF16),
                        pltpu.VMEM((2, 2 * tq, s), BF16)],
        compiler_params=_params("parallel", "parallel"),
        name="mixers",
    )(lam_p, gsub, *rope, u, u, u, u, u, u)


def _gla_body(gate_ref, wup_ref, bup_ref, gnorm_ref, q_ref, k_ref, v_ref, og_ref, o_ref,
              loga_ref, state_ref):
    s = q_ref.shape[1]
    pre = jnp.dot(gate_ref[0].astype(BF16), wup_ref[...], preferred_element_type=F32)
    loga_ref[...] = _log_sigmoid(pre + bup_ref[...]) * (1.0 / GLA_GATE_NORM)
    state_ref[...] = jnp.zeros_like(state_ref)

    rr = lax.broadcasted_iota(jnp.int32, (CHUNK, CHUNK), 0)
    cc = lax.broadcasted_iota(jnp.int32, (CHUNK, CHUNK), 1)
    causal = rr >= cc
    prefix = jnp.where(causal, 1.0, 0.0).astype(BF16)
    gnorm = gnorm_ref[...]

    def chunk(n):
        sl = pl.ds(pl.multiple_of(n * CHUNK, CHUNK), CHUNK)
        la = loga_ref[sl, :]
        hi = la.astype(BF16)
        r1 = la - hi.astype(F32)
        mid = r1.astype(BF16)
        lo = (r1 - mid.astype(F32)).astype(BF16)
        bcum = (jnp.dot(prefix, hi, preferred_element_type=F32)
                + jnp.dot(prefix, mid, preferred_element_type=F32)
                + jnp.dot(prefix, lo, preferred_element_type=F32))
        b_last = bcum[CHUNK - 1:CHUNK, :]
        q = q_ref[0, sl, :].astype(F32) * GLA_DK ** -0.5
        k = k_ref[0, sl, :].astype(F32)
        q_d = (q * jnp.exp(bcum)).astype(BF16)
        k_d = (k * jnp.exp(-bcum)).astype(BF16)
        k_l = (k * jnp.exp(b_last - bcum)).astype(BF16)
        decay = jnp.exp(b_last)
        for hd in range(GLA_HEADS):
            ks = slice(hd * GLA_DK, (hd + 1) * GLA_DK)
            vs = slice(hd * GLA_DV, (hd + 1) * GLA_DV)
            v = v_ref[0, sl, vs]
            att = lax.dot_general(q_d[:, ks], k_d[:, ks], _NT, preferred_element_type=F32)
            att = jnp.where(causal, att, 0.0).astype(BF16)
            st = state_ref[hd]
            o = (jnp.dot(att, v, preferred_element_type=F32)
                 + lax.dot_general(q_d[:, ks], st.astype(BF16), _NT, preferred_element_type=F32))
            state_ref[hd] = decay[:, ks] * st + lax.dot_general(
                v, k_l[:, ks], _TN, preferred_element_type=F32)
            og = og_ref[0, sl, vs].astype(F32)
            o_ref[0, sl, vs] = (_rms(o, gnorm) * (og * _sigmoid(og))).astype(BF16)

    group = math.gcd(GLA_CHUNKS_PER_TRIP, s // CHUNK)

    def trip(g, _):
        for t in range(group):
            chunk(g * group + t)
        return 0

    lax.fori_loop(0, s // CHUNK // group, trip, 0)


def _gla(u, gate, wup, bup, gnorm):
    b, s, _ = u.shape
    hk = GLA_HEADS * GLA_DK
    return pl.pallas_call(
        _gla_body,
        grid=(b,),
        in_specs=[
            pl.BlockSpec((1, s, LANES), lambda bi: (bi, 0, 0)),
            pl.BlockSpec((LANES, hk), lambda bi: (0, 0)),
            pl.BlockSpec((1, hk), lambda bi: (0, 0)),
            pl.BlockSpec((1, GLA_DV), lambda bi: (0, 0)),
            pl.BlockSpec((1, s, hk), lambda bi: (bi, 0, U_QB // hk)),
            pl.BlockSpec((1, s, hk), lambda bi: (bi, 0, U_KB // hk)),
            pl.BlockSpec((1, s, GLA_WIDTH), lambda bi: (bi, 0, U_VB // GLA_WIDTH)),
            pl.BlockSpec((1, s, GLA_WIDTH), lambda bi: (bi, 0, U_OB // GLA_WIDTH)),
        ],
        out_specs=pl.BlockSpec((1, s, GLA_WIDTH), lambda bi: (bi, 0, 0)),
        out_shape=jax.ShapeDtypeStruct((b, s, GLA_WIDTH), BF16),
        scratch_shapes=[pltpu.VMEM((s, hk), F32), pltpu.VMEM((GLA_HEADS, GLA_DV, GLA_DK), F32)],
        compiler_params=_params("parallel"),
        name="gla",
    )(gate, wup, bup, gnorm, u, u, u, u)


def _outproj_body(h_ref, oa_ref, ob_ref, oc_ref, w_ref, o_ref):
    b0, c0 = DA_WIDTH, DA_WIDTH + GLA_WIDTH
    o_ref[...] = (h_ref[...]
                  + jnp.dot(oa_ref[...], w_ref[:b0, :], preferred_element_type=F32)
                  + jnp.dot(ob_ref[...], w_ref[b0:c0, :], preferred_element_type=F32)
                  + jnp.dot(oc_ref[...], w_ref[c0:, :], preferred_element_type=F32))


def _outproj(h, oa, ob, oc, w, *, tm):
    m, d = h.shape

    def rows(width):
        return pl.BlockSpec((tm, width), lambda i: (i, 0))

    return pl.pallas_call(
        _outproj_body,
        grid=(m // tm,),
        in_specs=[rows(d), rows(DA_WIDTH), rows(GLA_WIDTH), rows(SB_WIDTH),
                  pl.BlockSpec(w.shape, lambda i: (0, 0))],
        out_specs=rows(d),
        out_shape=jax.ShapeDtypeStruct((m, d), F32),
        compiler_params=_params("parallel"),
        name="outproj",
    )(h, oa, ob, oc, w)


def _to_bf16_body(w_ref, o_ref):
    o_ref[...] = w_ref[0].astype(BF16)


def _layer_to_bf16(w, layer, *, br):
    _, r, c = w.shape
    assert r % br == 0
    return pl.pallas_call(
        _to_bf16_body,
        grid=(r // br,),
        in_specs=[pl.BlockSpec((1, br, c), lambda i: (layer, i, 0))],
        out_specs=pl.BlockSpec((br, c), lambda i: (i, 0)),
        out_shape=jax.ShapeDtypeStruct((r, c), BF16),
        compiler_params=_params("parallel"),
        name="to_bf16",
    )(w)


def _ple_body(h_ref, p_ref, g_ref, wg_ref, wp_ref, gf_ref, o_ref, *, final):
    x = h_ref[...]
    n = _rms(x, g_ref[...]).astype(BF16)
    gate = _sigmoid(jnp.dot(n, wg_ref[...], preferred_element_type=F32))
    emb = jnp.dot(p_ref[...].astype(BF16), wp_ref[...], preferred_element_type=F32)
    y = x + emb * gate
    o_ref[...] = _rms(y, gf_ref[...]) if final else y


def _ple(h, p, g, wg, wp, gf, *, tm, final):
    m, d = h.shape
    pd = p.shape[1]
    return pl.pallas_call(
        functools.partial(_ple_body, final=final),
        grid=(m // tm,),
        in_specs=[
            pl.BlockSpec((tm, d), lambda i: (i, 0)),
            pl.BlockSpec((tm, pd), lambda i: (i, 0)),
            pl.BlockSpec((1, d), lambda i: (0, 0)),
            pl.BlockSpec((d, d), lambda i: (0, 0)),
            pl.BlockSpec((pd, d), lambda i: (0, 0)),
            pl.BlockSpec((1, d), lambda i: (0, 0)),
        ],
        out_specs=pl.BlockSpec((tm, d), lambda i: (i, 0)),
        out_shape=jax.ShapeDtypeStruct((m, d), F32),
        compiler_params=_params("parallel"),
        name="ple",
    )(h, p, g, wg, wp, gf)


def _reorder_w_in(w):
    sizes = (DA_WIDTH, DA_WIDTH, DA_WIDTH, GLA_HEADS * GLA_DK, GLA_HEADS * GLA_DK, GLA_WIDTH,
             GLA_RANK, GLA_WIDTH, SB_WIDTH, SB_WIDTH, SB_WIDTH)
    parts, start = [], 0
    for size in sizes:
        parts.append(w[:, start:start + size])
        start += size
    qa, ka, va, qb, kb, vb, gb, ob, qc, kc, vc = parts
    main = jnp.concatenate([qa, ka, va, qb, vb, ob, kb, qc, kc, vc], axis=1).astype(BF16)
    gate = jnp.pad(gb, ((0, 0), (0, LANES - GLA_RANK))).astype(BF16)
    return main, gate


def kernel(x, p, positions, g_ffn1, w_ffn1_gate, w_ffn1_up, w_ffn1_down, g_mix, w_in, w_gla_gate_up, b_gla_gate, g_gla_norm, lambda_q1, lambda_k1, lambda_q2, lambda_k2, g_diff_norm, w_out, g_ffn2, w_ffn2_gate, w_ffn2_up, w_ffn2_down, g_ple, w_ple_gate, w_ple_proj, g_final):
    b, s, d = x.shape
    m = b * s
    tm = min(512, m)
    tm_big = min(1024, m)
    tq = min(256, s)
    assert m % tm_big == 0 and m % tm == 0 and s % tq == 0 and tq % CHUNK == 0

    inv_freq = ROPE_THETA ** (-jnp.arange(0, ROPE_DIMS, 2, dtype=F32) / ROPE_DIMS)
    freq_lanes = jnp.tile(inv_freq, LANES // inv_freq.shape[0]).reshape(1, LANES)
    pos_lanes = jnp.broadcast_to(positions.astype(F32).reshape(m, 1), (m, LANES))
    rope_c, rope_s1, rope_s2 = _rope_tables(pos_lanes, freq_lanes, tm=tm)
    rope = [t.reshape(b, s, LANES) for t in (rope_c, rope_s1, rope_s2)]

    def ffn_weights(wg, wu, wd, i):
        return (_layer_to_bf16(wg, i, br=256), _layer_to_bf16(wu, i, br=256),
                _layer_to_bf16(wd, i, br=512))

    h = x.reshape(m, d)
    for i in range(DEPTH):
        row = lambda a: a[i].reshape(1, -1)
        h = _ffn(h, row(g_ffn1), *ffn_weights(w_ffn1_gate, w_ffn1_up, w_ffn1_down, i),
                 tm=tm_big, tf=512)

        w_main, w_gate = _reorder_w_in(w_in[i])
        u, gate_low = _inproj(h, row(g_mix), w_main, w_gate, tm=tm_big, tn=1024)
        u3 = u.reshape(b, s, U_WIDTH)

        lam_init = 0.8 - 0.6 * math.exp(-0.3 * i)
        lam_p = jnp.stack([lambda_q1[i], lambda_k1[i], lambda_q2[i], lambda_k2[i]])
        o_a, o_c = _mixers(u3, rope, lam_p, row(g_diff_norm), tq=tq, lam_init=lam_init)
        wup = jnp.pad(w_gla_gate_up[i], ((0, LANES - GLA_RANK), (0, 0))).astype(BF16)
        o_b = _gla(u3, gate_low.reshape(b, s, LANES), wup, row(b_gla_gate), row(g_gla_norm))

        h = _outproj(h, o_a.reshape(m, DA_WIDTH), o_b.reshape(m, GLA_WIDTH),
                     o_c.reshape(m, SB_WIDTH), _layer_to_bf16(w_out, i, br=512), tm=tm)

        h = _ffn(h, row(g_ffn2), *ffn_weights(w_ffn2_gate, w_ffn2_up, w_ffn2_down, i),
                 tm=tm_big, tf=512)
        h = _ple(h, p[i].reshape(m, PLE_DIM), row(g_ple), _layer_to_bf16(w_ple_gate, i, br=512),
                 w_ple_proj[i].astype(BF16), g_final.reshape(1, -1), tm=tm,
                 final=(i == DEPTH - 1))
    return h.reshape(b, s, d)
```

```python
import functools
import math

import jax
import jax.numpy as jnp
from jax import lax
from jax.experimental import pallas as pl
from jax.experimental.pallas import tpu as pltpu

F32 = jnp.float32
BF16 = jnp.bfloat16

D_MODEL = 2048
DEPTH = 2
CHUNK = 64
DA_HEADS = 4
DA_QK = 64
DA_V = 128
DA_WIDTH = 512
GLA_HEADS = 4
GLA_DK = 128
GLA_DV = 256
GLA_RANK = 16
GLA_GATE_NORM = 16.0
GLA_WIDTH = 1024
SB_HEADS = 8
SB_DIM = 64
SB_WIDTH = 512
ROPE_THETA = 500000.0
ROPE_DIMS = 16
D_FF = 5632
PLE_DIM = 256
EPS = 1e-6

LANES = 128
U_QA, U_KA, U_VA, U_QB, U_VB, U_OB, U_KB, U_QC, U_KC, U_VC = (
    0, 512, 1024, 1536, 2048, 3072, 4096, 4608, 5120, 5632)
U_WIDTH = 6144

VMEM_LIMIT = 56 * 1024 * 1024
GLA_CHUNKS_PER_TRIP = 4

_NT = (((1,), (1,)), ((), ()))
_TN = (((0,), (0,)), ((), ()))


def _params(*sem):
    return pltpu.CompilerParams(dimension_semantics=sem, vmem_limit_bytes=VMEM_LIMIT)


def _rms(x, g):
    return x * lax.rsqrt(jnp.mean(x * x, axis=-1, keepdims=True) + EPS) * g


def _log_sigmoid(z):
    return jnp.minimum(z, 0.0) - jnp.log(1.0 + jnp.exp(-jnp.abs(z)))


def _sigmoid(z):
    return 1.0 / (1.0 + jnp.exp(-z))


def _ffn_body(h_ref, g_ref, wg_ref, wu_ref, wd_ref, o_ref, n_ref):
    @pl.when(pl.program_id(1) == 0)
    def _():
        x = h_ref[...]
        n_ref[...] = _rms(x, g_ref[...]).astype(BF16)
        o_ref[...] = x

    n = n_ref[...]
    a = jnp.dot(n, wg_ref[...], preferred_element_type=F32)
    b = jnp.dot(n, wu_ref[...], preferred_element_type=F32)
    act = (0.5 * a * _sigmoid(a) * b).astype(BF16)
    o_ref[...] += jnp.dot(act, wd_ref[...], preferred_element_type=F32)


def _ffn(h, g, wg, wu, wd, *, tm, tf):
    m, d = h.shape
    f = wg.shape[1]
    return pl.pallas_call(
        _ffn_body,
        grid=(m // tm, f // tf),
        in_specs=[
            pl.BlockSpec((tm, d), lambda i, j: (i, 0)),
            pl.BlockSpec((1, d), lambda i, j: (0, 0)),
            pl.BlockSpec((d, tf), lambda i, j: (0, j)),
            pl.BlockSpec((d, tf), lambda i, j: (0, j)),
            pl.BlockSpec((tf, d), lambda i, j: (j, 0)),
        ],
        out_specs=pl.BlockSpec((tm, d), lambda i, j: (i, 0)),
        out_shape=jax.ShapeDtypeStruct((m, d), F32),
        scratch_shapes=[pltpu.VMEM((tm, d), BF16)],
        compiler_params=_params("parallel", "arbitrary"),
        name="ffn",
    )(h, g, wg, wu, wd)


def _inproj_body(h_ref, g_ref, w_ref, wgate_ref, u_ref, gate_ref, n_ref):
    @pl.when(pl.program_id(1) == 0)
    def _():
        n = _rms(h_ref[...], g_ref[...]).astype(BF16)
        n_ref[...] = n
        gate_ref[...] = jnp.dot(n, wgate_ref[...], preferred_element_type=F32)

    u_ref[...] = jnp.dot(n_ref[...], w_ref[...], preferred_element_type=F32).astype(BF16)


def _inproj(h, g, w, wgate, *, tm, tn):
    m, d = h.shape
    n = w.shape[1]
    return pl.pallas_call(
        _inproj_body,
        grid=(m // tm, n // tn),
        in_specs=[
            pl.BlockSpec((tm, d), lambda i, j: (i, 0)),
            pl.BlockSpec((1, d), lambda i, j: (0, 0)),
            pl.BlockSpec((d, tn), lambda i, j: (0, j)),
            pl.BlockSpec((d, LANES), lambda i, j: (0, 0)),
        ],
        out_specs=[
            pl.BlockSpec((tm, tn), lambda i, j: (i, j)),
            pl.BlockSpec((tm, LANES), lambda i, j: (i, 0)),
        ],
        out_shape=[
            jax.ShapeDtypeStruct((m, n), BF16),
            jax.ShapeDtypeStruct((m, LANES), F32),
        ],
        scratch_shapes=[pltpu.VMEM((tm, d), BF16)],
        compiler_params=_params("parallel", "arbitrary"),
        name="inproj",
    )(h, g, w, wgate)


def _rope_table_body(pos_ref, freq_ref, c_ref, s1_ref, s2_ref):
    ang = pos_ref[...] * freq_ref[...]
    c = jnp.cos(ang)
    s = jnp.sin(ang)
    dim = lax.broadcasted_iota(jnp.int32, ang.shape, 1) % DA_QK
    half = ROPE_DIMS // 2
    c_ref[...] = jnp.where(dim < ROPE_DIMS, c, 1.0)
    s1_ref[...] = jnp.where(dim < half, -s, 0.0)
    s2_ref[...] = jnp.where((dim >= half) & (dim < ROPE_DIMS), s, 0.0)


def _rope_tables(pos_lanes, freq_lanes, *, tm):
    m = pos_lanes.shape[0]
    spec = pl.BlockSpec((tm, LANES), lambda i: (i, 0))
    return pl.pallas_call(
        _rope_table_body,
        grid=(m // tm,),
        in_specs=[spec, pl.BlockSpec((1, LANES), lambda i: (0, 0))],
        out_specs=[spec, spec, spec],
        out_shape=[jax.ShapeDtypeStruct((m, LANES), F32)] * 3,
        compiler_params=_params("parallel"),
        name="rope_tables",
    )(pos_lanes, freq_lanes)


def _split_lane_halves(x):
    lane = lax.broadcasted_iota(jnp.int32, x.shape, 1)
    zero = jnp.zeros_like(x)
    return jnp.concatenate(
        [jnp.where(lane < LANES // 2, x, zero), jnp.where(lane >= LANES // 2, x, zero)], axis=0)


def _fold_lanes(x, op):
    out = x[:, :LANES]
    for g in range(1, x.shape[1] // LANES):
        out = op(out, x[:, g * LANES:(g + 1) * LANES])
    return out


def _mixers_body(lam_ref, gsub_ref, cos_ref, sin_lo_ref, sin_hi_ref, qa_ref, ka_ref, va_ref,
                 qc_ref, kc_ref, vc_ref, oa_ref, oc_ref, qr_ref, kr_ref, s_ref, p_ref, a_ref,
                 *, tq, lam_init):
    nq = qa_ref.shape[1] // tq
    log2e = math.log2(math.e)
    row = lax.broadcasted_iota(jnp.int32, (2 * tq, tq), 0) % tq
    col = lax.broadcasted_iota(jnp.int32, (2 * tq, tq), 1)
    lane = lax.broadcasted_iota(jnp.int32, (tq, LANES), 1)

    half = ROPE_DIMS // 2
    cos, sin_lo, sin_hi = cos_ref[0], sin_lo_ref[0], sin_hi_ref[0]

    def rot(x):
        return (x * cos + pltpu.roll(x, LANES - half, 1) * sin_lo
                + pltpu.roll(x, half, 1) * sin_hi)

    qr_ref[...] = (rot(qa_ref[0].astype(F32)) * (DA_QK ** -0.5 * log2e)).astype(BF16)
    kr_ref[...] = rot(ka_ref[0].astype(F32)).astype(BF16)
    lam_p = lam_ref[...]
    lam = (jnp.exp(jnp.sum(lam_p[0:1] * lam_p[1:2], axis=-1, keepdims=True))
           - jnp.exp(jnp.sum(lam_p[2:3] * lam_p[3:4], axis=-1, keepdims=True)) + lam_init)
    gsub = gsub_ref[...]
    chunk_mask = col // CHUNK <= row // CHUNK

    def diff_block(c):
        buf = c % 2
        qs = _split_lane_halves(qr_ref[c * tq:(c + 1) * tq, :])
        mx = None
        for j in range(c + 1):
            ks = slice(j * tq, (j + 1) * tq)
            s = lax.dot_general(qs, kr_ref[ks, :], _NT, preferred_element_type=F32)
            if j == c:
                s = jnp.where(chunk_mask, s, -jnp.inf)
            s_ref[buf, :, ks] = s
            mj = _fold_lanes(s, jnp.maximum)
            mx = mj if mx is None else jnp.maximum(mx, mj)
            yield
        m = jnp.max(mx, axis=-1, keepdims=True)
        ls = None
        for j in range(c + 1):
            ks = slice(j * tq, (j + 1) * tq)
            p = jnp.exp2(s_ref[buf, :, ks] - m)
            p_ref[buf, :, ks] = p.astype(BF16)
            pj = _fold_lanes(p, jnp.add)
            ls = pj if ls is None else ls + pj
            yield
        l = jnp.sum(ls, axis=-1, keepdims=True)
        n = (c + 1) * tq
        acc = jnp.dot(p_ref[buf, :, :n], va_ref[0, :n, :], preferred_element_type=F32)
        o = acc / l
        o = o[:tq] - lam * o[tq:]
        oa_ref[0, c * tq:(c + 1) * tq, :] = (_rms(o, gsub) * (1.0 - lam_init)).astype(BF16)

    later = jnp.where(row > col, 1.0, 0.0).astype(BF16)
    strictly_before = col < row

    def sb_block(c):
        buf = c % 2
        q = qc_ref[0, c * tq:(c + 1) * tq, :].astype(F32) * (SB_DIM ** -0.5 * log2e)
        qs = _split_lane_halves(q.astype(BF16))
        carry = None
        for j in range(c, -1, -1):
            ks = slice(j * tq, (j + 1) * tq)
            z = lax.dot_general(qs, kc_ref[0, ks, :], _NT, preferred_element_type=F32)
            softplus = jnp.log(1.0 + jnp.exp2(-jnp.abs(z))) * log2e
            log_beta = jnp.minimum(z, 0.0) - softplus
            log_1m = log_beta - z
            if j == c:
                log_1m = jnp.where(strictly_before, log_1m, 0.0)
            hi = log_1m.astype(BF16)
            lo = (log_1m - hi.astype(F32)).astype(BF16)
            tail = jnp.dot(jnp.concatenate([hi, lo], axis=1), later,
                           preferred_element_type=F32)
            expo = log_beta + tail
            if carry is not None:
                expo = expo + carry
            a = jnp.exp2(expo)
            if j == c:
                a = jnp.where(strictly_before, a, 0.0)
            a_ref[buf, :, ks] = a.astype(BF16)
            rs = jnp.sum(_fold_lanes(log_1m, jnp.add), axis=-1, keepdims=True)
            carry = rs if carry is None else carry + rs
            yield
        n = (c + 1) * tq
        acc = jnp.dot(a_ref[buf, :, :n], vc_ref[0, :n, :], preferred_element_type=F32)
        oc_ref[0, c * tq:(c + 1) * tq, :] = jnp.where(
            lane < LANES // 2, acc[:tq], acc[tq:]).astype(BF16)

    for c in range(nq):
        sweeps = [diff_block(c), sb_block(c)]
        while sweeps:
            for sweep in list(sweeps):
                if next(sweep, "done") == "done":
                    sweeps.remove(sweep)


def _mixers(u, rope, lam_p, gsub, *, tq, lam_init):
    b, s, _ = u.shape
    assert DA_HEADS == SB_WIDTH // LANES

    def slab(offset):
        return pl.BlockSpec((1, s, LANES), lambda bi, hi: (bi, 0, offset // LANES + hi))

    table = pl.BlockSpec((1, s, LANES), lambda bi, hi: (bi, 0, 0))
    return pl.pallas_call(
        functools.partial(_mixers_body, tq=tq, lam_init=lam_init),
        grid=(b, DA_HEADS),
        in_specs=[
            pl.BlockSpec((4, DA_QK), lambda bi, hi: (0, 0)),
            pl.BlockSpec((1, DA_V), lambda bi, hi: (0, 0)),
            table, table, table,
            slab(U_QA), slab(U_KA), slab(U_VA), slab(U_QC), slab(U_KC), slab(U_VC),
        ],
        out_specs=[slab(0), slab(0)],
        out_shape=[jax.ShapeDtypeStruct((b, s, DA_WIDTH), BF16),
                   jax.ShapeDtypeStruct((b, s, SB_WIDTH), BF16)],
        scratch_shapes=[pltpu.VMEM((s, LANES), BF16), pltpu.VMEM((s, LANES), BF16),
                        pltpu.VMEM((2, 2 * tq, s), F32), pltpu.VMEM((2, 2 * tq, s), BF16),
                        pltpu.VMEM((2, 2 * tq, s), BF16)],
        compiler_params=_params("parallel", "parallel"),
        name="mixers",
    )(lam_p, gsub, *rope, u, u, u, u, u, u)


def _gla_body(gate_ref, wup_ref, bup_ref, gnorm_ref, q_ref, k_ref, v_ref, og_ref, o_ref,
              loga_ref, state_ref):
    s = q_ref.shape[1]
    pre = jnp.dot(gate_ref[0].astype(BF16), wup_ref[...], preferred_element_type=F32)
    loga_ref[...] = _log_sigmoid(pre + bup_ref[...]) * (1.0 / GLA_GATE_NORM)
    state_ref[...] = jnp.zeros_like(state_ref)

    rr = lax.broadcasted_iota(jnp.int32, (CHUNK, CHUNK), 0)
    cc = lax.broadcasted_iota(jnp.int32, (CHUNK, CHUNK), 1)
    causal = rr >= cc
    prefix = jnp.where(causal, 1.0, 0.0).astype(BF16)
    gnorm = gnorm_ref[...]

    def chunk(n):
        sl = pl.ds(pl.multiple_of(n * CHUNK, CHUNK), CHUNK)
        la = loga_ref[sl, :]
        hi = la.astype(BF16)
        r1 = la - hi.astype(F32)
        mid = r1.astype(BF16)
        lo = (r1 - mid.astype(F32)).astype(BF16)
        bcum = (jnp.dot(prefix, hi, preferred_element_type=F32)
                + jnp.dot(prefix, mid, preferred_element_type=F32)
                + jnp.dot(prefix, lo, preferred_element_type=F32))
        b_last = bcum[CHUNK - 1:CHUNK, :]
        q = q_ref[0, sl, :].astype(F32) * GLA_DK ** -0.5
        k = k_ref[0, sl, :].astype(F32)
        q_d = (q * jnp.exp(bcum)).astype(BF16)
        k_d = (k * jnp.exp(-bcum)).astype(BF16)
        k_l = (k * jnp.exp(b_last - bcum)).astype(BF16)
        decay = jnp.exp(b_last)

        def head(hd):
            ks = slice(hd * GLA_DK, (hd + 1) * GLA_DK)
            vs = slice(hd * GLA_DV, (hd + 1) * GLA_DV)
            v = v_ref[0, sl, vs]
            att = lax.dot_general(q_d[:, ks], k_d[:, ks], _NT, preferred_element_type=F32)
            att = jnp.where(causal, att, 0.0).astype(BF16)
            yield
            st = state_ref[hd]
            o = (jnp.dot(att, v, preferred_element_type=F32)
                 + lax.dot_general(q_d[:, ks], st.astype(BF16), _NT, preferred_element_type=F32))
            yield
            state_ref[hd] = decay[:, ks] * st + lax.dot_general(
                v, k_l[:, ks], _TN, preferred_element_type=F32)
            yield
            og = og_ref[0, sl, vs].astype(F32)
            o_ref[0, sl, vs] = (_rms(o, gnorm) * (og * _sigmoid(og))).astype(BF16)

        heads = [head(hd) for hd in range(GLA_HEADS)]
        while heads:
            for h in list(heads):
                if next(h, "done") == "done":
                    heads.remove(h)

    group = math.gcd(GLA_CHUNKS_PER_TRIP, s // CHUNK)

    def trip(g, _):
        for t in range(group):
            chunk(g * group + t)
        return 0

    lax.fori_loop(0, s // CHUNK // group, trip, 0)


def _gla(u, gate, wup, bup, gnorm):
    b, s, _ = u.shape
    hk = GLA_HEADS * GLA_DK
    return pl.pallas_call(
        _gla_body,
        grid=(b,),
        in_specs=[
            pl.BlockSpec((1, s, LANES), lambda bi: (bi, 0, 0)),
            pl.BlockSpec((LANES, hk), lambda bi: (0, 0)),
            pl.BlockSpec((1, hk), lambda bi: (0, 0)),
            pl.BlockSpec((1, GLA_DV), lambda bi: (0, 0)),
            pl.BlockSpec((1, s, hk), lambda bi: (bi, 0, U_QB // hk)),
            pl.BlockSpec((1, s, hk), lambda bi: (bi, 0, U_KB // hk)),
            pl.BlockSpec((1, s, GLA_WIDTH), lambda bi: (bi, 0, U_VB // GLA_WIDTH)),
            pl.BlockSpec((1, s, GLA_WIDTH), lambda bi: (bi, 0, U_OB // GLA_WIDTH)),
        ],
        out_specs=pl.BlockSpec((1, s, GLA_WIDTH), lambda bi: (bi, 0, 0)),
        out_shape=jax.ShapeDtypeStruct((b, s, GLA_WIDTH), BF16),
        scratch_shapes=[pltpu.VMEM((s, hk), F32), pltpu.VMEM((GLA_HEADS, GLA_DV, GLA_DK), F32)],
        compiler_params=_params("parallel"),
        name="gla",
    )(gate, wup, bup, gnorm, u, u, u, u)


def _outproj_body(h_ref, oa_ref, ob_ref, oc_ref, w_ref, o_ref):
    b0, c0 = DA_WIDTH, DA_WIDTH + GLA_WIDTH
    o_ref[...] = (h_ref[...]
                  + jnp.dot(oa_ref[...], w_ref[:b0, :], preferred_element_type=F32)
                  + jnp.dot(ob_ref[...], w_ref[b0:c0, :], preferred_element_type=F32)
                  + jnp.dot(oc_ref[...], w_ref[c0:, :], preferred_element_type=F32))


def _outproj(h, oa, ob, oc, w, *, tm):
    m, d = h.shape

    def rows(width):
        return pl.BlockSpec((tm, width), lambda i: (i, 0))

    return pl.pallas_call(
        _outproj_body,
        grid=(m // tm,),
        in_specs=[rows(d), rows(DA_WIDTH), rows(GLA_WIDTH), rows(SB_WIDTH),
                  pl.BlockSpec(w.shape, lambda i: (0, 0))],
        out_specs=rows(d),
        out_shape=jax.ShapeDtypeStruct((m, d), F32),
        compiler_params=_params("parallel"),
        name="outproj",
    )(h, oa, ob, oc, w)


def _to_bf16_body(w_ref, o_ref):
    o_ref[...] = w_ref[0].astype(BF16)


def _layer_to_bf16(w, layer, *, br):
    _, r, c = w.shape
    assert r % br == 0
    return pl.pallas_call(
        _to_bf16_body,
        grid=(r // br,),
        in_specs=[pl.BlockSpec((1, br, c), lambda i: (layer, i, 0))],
        out_specs=pl.BlockSpec((br, c), lambda i: (i, 0)),
        out_shape=jax.ShapeDtypeStruct((r, c), BF16),
        compiler_params=_params("parallel"),
        name="to_bf16",
    )(w)


def _ple_body(h_ref, p_ref, g_ref, wg_ref, wp_ref, gf_ref, o_ref, *, final):
    x = h_ref[...]
    n = _rms(x, g_ref[...]).astype(BF16)
    gate = _sigmoid(jnp.dot(n, wg_ref[...], preferred_element_type=F32))
    emb = jnp.dot(p_ref[...].astype(BF16), wp_ref[...], preferred_element_type=F32)
    y = x + emb * gate
    o_ref[...] = _rms(y, gf_ref[...]) if final else y


def _ple(h, p, g, wg, wp, gf, *, tm, final):
    m, d = h.shape
    pd = p.shape[1]
    return pl.pallas_call(
        functools.partial(_ple_body, final=final),
        grid=(m // tm,),
        in_specs=[
            pl.BlockSpec((tm, d), lambda i: (i, 0)),
            pl.BlockSpec((tm, pd), lambda i: (i, 0)),
            pl.BlockSpec((1, d), lambda i: (0, 0)),
            pl.BlockSpec((d, d), lambda i: (0, 0)),
            pl.BlockSpec((pd, d), lambda i: (0, 0)),
            pl.BlockSpec((1, d), lambda i: (0, 0)),
        ],
        out_specs=pl.BlockSpec((tm, d), lambda i: (i, 0)),
        out_shape=jax.ShapeDtypeStruct((m, d), F32),
        compiler_params=_params("parallel"),
        name="ple",
    )(h, p, g, wg, wp, gf)


def _reorder_w_in(w):
    sizes = (DA_WIDTH, DA_WIDTH, DA_WIDTH, GLA_HEADS * GLA_DK, GLA_HEADS * GLA_DK, GLA_WIDTH,
             GLA_RANK, GLA_WIDTH, SB_WIDTH, SB_WIDTH, SB_WIDTH)
    parts, start = [], 0
    for size in sizes:
        parts.append(w[:, start:start + size])
        start += size
    qa, ka, va, qb, kb, vb, gb, ob, qc, kc, vc = parts
    main = jnp.concatenate([qa, ka, va, qb, vb, ob, kb, qc, kc, vc], axis=1).astype(BF16)
    gate = jnp.pad(gb, ((0, 0), (0, LANES - GLA_RANK))).astype(BF16)
    return main, gate


def kernel(x, p, positions, g_ffn1, w_ffn1_gate, w_ffn1_up, w_ffn1_down, g_mix, w_in, w_gla_gate_up, b_gla_gate, g_gla_norm, lambda_q1, lambda_k1, lambda_q2, lambda_k2, g_diff_norm, w_out, g_ffn2, w_ffn2_gate, w_ffn2_up, w_ffn2_down, g_ple, w_ple_gate, w_ple_proj, g_final):
    b, s, d = x.shape
    m = b * s
    tm = min(512, m)
    tm_big = min(1024, m)
    tq = min(256, s)
    assert m % tm_big == 0 and m % tm == 0 and s % tq == 0 and tq % CHUNK == 0

    inv_freq = ROPE_THETA ** (-jnp.arange(0, ROPE_DIMS, 2, dtype=F32) / ROPE_DIMS)
    freq_lanes = jnp.tile(inv_freq, LANES // inv_freq.shape[0]).reshape(1, LANES)
    pos_lanes = jnp.broadcast_to(positions.astype(F32).reshape(m, 1), (m, LANES))
    rope_c, rope_s1, rope_s2 = _rope_tables(pos_lanes, freq_lanes, tm=tm)
    rope = [t.reshape(b, s, LANES) for t in (rope_c, rope_s1, rope_s2)]

    def ffn_weights(wg, wu, wd, i):
        return (_layer_to_bf16(wg, i, br=256), _layer_to_bf16(wu, i, br=256),
                _layer_to_bf16(wd, i, br=512))

    h = x.reshape(m, d)
    for i in range(DEPTH):
        row = lambda a: a[i].reshape(1, -1)
        h = _ffn(h, row(g_ffn1), *ffn_weights(w_ffn1_gate, w_ffn1_up, w_ffn1_down, i),
                 tm=tm_big, tf=512)

        w_main, w_gate = _reorder_w_in(w_in[i])
        u, gate_low = _inproj(h, row(g_mix), w_main, w_gate, tm=tm_big, tn=2048)
        u3 = u.reshape(b, s, U_WIDTH)

        lam_init = 0.8 - 0.6 * math.exp(-0.3 * i)
        lam_p = jnp.stack([lambda_q1[i], lambda_k1[i], lambda_q2[i], lambda_k2[i]])
        o_a, o_c = _mixers(u3, rope, lam_p, row(g_diff_norm), tq=tq, lam_init=lam_init)
        wup = jnp.pad(w_gla_gate_up[i], ((0, LANES - GLA_RANK), (0, 0))).astype(BF16)
        o_b = _gla(u3, gate_low.reshape(b, s, LANES), wup, row(b_gla_gate), row(g_gla_norm))

        h = _outproj(h, o_a.reshape(m, DA_WIDTH), o_b.reshape(m, GLA_WIDTH),
                     o_c.reshape(m, SB_WIDTH), _layer_to_bf16(w_out, i, br=512), tm=tm)

        h = _ffn(h, row(g_ffn2), *ffn_weights(w_ffn2_gate, w_ffn2_up, w_ffn2_down, i),
                 tm=tm_big, tf=512)
        h = _ple(h, p[i].reshape(m, PLE_DIM), row(g_ple), _layer_to_bf16(w_ple_gate, i, br=512),
                 w_ple_proj[i].astype(BF16), g_final.reshape(1, -1), tm=tm,
                 final=(i == DEPTH - 1))
    return h.reshape(b, s, d)
```

```python
import functools
import math

import jax
import jax.numpy as jnp
from jax import lax
from jax.experimental import pallas as pl
from jax.experimental.pallas import tpu as pltpu

F32 = jnp.float32
BF16 = jnp.bfloat16

D_MODEL = 2048
DEPTH = 2
CHUNK = 64
DA_HEADS = 4
DA_QK = 64
DA_V = 128
DA_WIDTH = 512
GLA_HEADS = 4
GLA_DK = 128
GLA_DV = 256
GLA_RANK = 16
GLA_GATE_NORM = 16.0
GLA_WIDTH = 1024
SB_HEADS = 8
SB_DIM = 64
SB_WIDTH = 512
ROPE_THETA = 500000.0
ROPE_DIMS = 16
D_FF = 5632
PLE_DIM = 256
EPS = 1e-6

LANES = 128
U_QA, U_KA, U_VA, U_QB, U_VB, U_OB, U_KB, U_QC, U_KC, U_VC = (
    0, 512, 1024, 1536, 2048, 3072, 4096, 4608, 5120, 5632)
U_WIDTH = 6144

VMEM_LIMIT = 56 * 1024 * 1024
GLA_CHUNKS_PER_TRIP = 4

_NT = (((1,), (1,)), ((), ()))
_TN = (((0,), (0,)), ((), ()))


def _params(*sem):
    return pltpu.CompilerParams(dimension_semantics=sem, vmem_limit_bytes=VMEM_LIMIT)


def _rms(x, g):
    return x * lax.rsqrt(jnp.mean(x * x, axis=-1, keepdims=True) + EPS) * g


def _log_sigmoid(z):
    return jnp.minimum(z, 0.0) - jnp.log(1.0 + jnp.exp(-jnp.abs(z)))


def _sigmoid(z):
    return 1.0 / (1.0 + jnp.exp(-z))


def _ffn_body(h_ref, g_ref, wg_ref, wu_ref, wd_ref, o_ref, n_ref):
    @pl.when(pl.program_id(1) == 0)
    def _():
        x = h_ref[...]
        n_ref[...] = _rms(x, g_ref[...]).astype(BF16)
        o_ref[...] = x

    n = n_ref[...]
    a = jnp.dot(n, wg_ref[...], preferred_element_type=F32)
    b = jnp.dot(n, wu_ref[...], preferred_element_type=F32)
    act = (0.5 * a * _sigmoid(a) * b).astype(BF16)
    o_ref[...] += jnp.dot(act, wd_ref[...], preferred_element_type=F32)


def _ffn(h, g, wg, wu, wd, *, tm, tf):
    m, d = h.shape
    f = wg.shape[1]
    return pl.pallas_call(
        _ffn_body,
        grid=(m // tm, f // tf),
        in_specs=[
            pl.BlockSpec((tm, d), lambda i, j: (i, 0)),
            pl.BlockSpec((1, d), lambda i, j: (0, 0)),
            pl.BlockSpec((d, tf), lambda i, j: (0, j)),
            pl.BlockSpec((d, tf), lambda i, j: (0, j)),
            pl.BlockSpec((tf, d), lambda i, j: (j, 0)),
        ],
        out_specs=pl.BlockSpec((tm, d), lambda i, j: (i, 0)),
        out_shape=jax.ShapeDtypeStruct((m, d), F32),
        scratch_shapes=[pltpu.VMEM((tm, d), BF16)],
        compiler_params=_params("parallel", "arbitrary"),
        name="ffn",
    )(h, g, wg, wu, wd)


def _inproj_body(h_ref, g_ref, w_ref, wgate_ref, u_ref, gate_ref, n_ref):
    @pl.when(pl.program_id(1) == 0)
    def _():
        n = _rms(h_ref[...], g_ref[...]).astype(BF16)
        n_ref[...] = n
        gate_ref[...] = jnp.dot(n, wgate_ref[...], preferred_element_type=F32)

    u_ref[...] = jnp.dot(n_ref[...], w_ref[...], preferred_element_type=F32).astype(BF16)


def _inproj(h, g, w, wgate, *, tm, tn):
    m, d = h.shape
    n = w.shape[1]
    return pl.pallas_call(
        _inproj_body,
        grid=(m // tm, n // tn),
        in_specs=[
            pl.BlockSpec((tm, d), lambda i, j: (i, 0)),
            pl.BlockSpec((1, d), lambda i, j: (0, 0)),
            pl.BlockSpec((d, tn), lambda i, j: (0, j)),
            pl.BlockSpec((d, LANES), lambda i, j: (0, 0)),
        ],
        out_specs=[
            pl.BlockSpec((tm, tn), lambda i, j: (i, j)),
            pl.BlockSpec((tm, LANES), lambda i, j: (i, 0)),
        ],
        out_shape=[
            jax.ShapeDtypeStruct((m, n), BF16),
            jax.ShapeDtypeStruct((m, LANES), F32),
        ],
        scratch_shapes=[pltpu.VMEM((tm, d), BF16)],
        compiler_params=_params("parallel", "arbitrary"),
        name="inproj",
    )(h, g, w, wgate)


def _rope_table_body(pos_ref, freq_ref, c_ref, s1_ref, s2_ref):
    ang = pos_ref[...] * freq_ref[...]
    c = jnp.cos(ang)
    s = jnp.sin(ang)
    dim = lax.broadcasted_iota(jnp.int32, ang.shape, 1) % DA_QK
    half = ROPE_DIMS // 2
    c_ref[...] = jnp.where(dim < ROPE_DIMS, c, 1.0)
    s1_ref[...] = jnp.where(dim < half, -s, 0.0)
    s2_ref[...] = jnp.where((dim >= half) & (dim < ROPE_DIMS), s, 0.0)


def _rope_tables(pos_lanes, freq_lanes, *, tm):
    m = pos_lanes.shape[0]
    spec = pl.BlockSpec((tm, LANES), lambda i: (i, 0))
    return pl.pallas_call(
        _rope_table_body,
        grid=(m // tm,),
        in_specs=[spec, pl.BlockSpec((1, LANES), lambda i: (0, 0))],
        out_specs=[spec, spec, spec],
        out_shape=[jax.ShapeDtypeStruct((m, LANES), F32)] * 3,
        compiler_params=_params("parallel"),
        name="rope_tables",
    )(pos_lanes, freq_lanes)


def _fold_lanes(x, op):
    out = x[:, :LANES]
    for g in range(1, x.shape[1] // LANES):
        out = op(out, x[:, g * LANES:(g + 1) * LANES])
    return out


def _mixers_body(lam_ref, gsub_ref, cos_ref, sin_lo_ref, sin_hi_ref, qa_ref, ka_ref, va_ref,
                 qc_ref, kc_ref, vc_ref, oa_ref, oc_ref, qr_ref, kr_ref, s_ref, p_ref, a_ref,
                 da_ref, sb_ref, *, tq, lam_init):
    nq = qa_ref.shape[1] // tq
    log2e = math.log2(math.e)
    row = lax.broadcasted_iota(jnp.int32, (tq, tq), 0)
    col = lax.broadcasted_iota(jnp.int32, (tq, tq), 1)
    lane = lax.broadcasted_iota(jnp.int32, (tq, LANES), 1)
    half_lanes = (lane < LANES // 2, lane >= LANES // 2)

    half = ROPE_DIMS // 2
    cos, sin_lo, sin_hi = cos_ref[0], sin_lo_ref[0], sin_hi_ref[0]

    def rot(x):
        return (x * cos + pltpu.roll(x, LANES - half, 1) * sin_lo
                + pltpu.roll(x, half, 1) * sin_hi)

    qr_ref[...] = (rot(qa_ref[0].astype(F32)) * (DA_QK ** -0.5 * log2e)).astype(BF16)
    kr_ref[...] = rot(ka_ref[0].astype(F32)).astype(BF16)
    lam_p = lam_ref[...]
    lam = (jnp.exp(jnp.sum(lam_p[0:1] * lam_p[1:2], axis=-1, keepdims=True))
           - jnp.exp(jnp.sum(lam_p[2:3] * lam_p[3:4], axis=-1, keepdims=True)) + lam_init)
    gsub = gsub_ref[...]
    chunk_mask = col // CHUNK <= row // CHUNK

    def diff_sweep(c, hf):
        buf = c % 2
        rows = slice(hf * tq, (hf + 1) * tq)
        q = qr_ref[c * tq:(c + 1) * tq, :]
        qs = jnp.where(half_lanes[hf], q, jnp.zeros_like(q))
        mx = None
        for j in range(c + 1):
            ks = slice(j * tq, (j + 1) * tq)
            s = lax.dot_general(qs, kr_ref[ks, :], _NT, preferred_element_type=F32)
            if j == c:
                s = jnp.where(chunk_mask, s, -jnp.inf)
            s_ref[buf, rows, ks] = s
            mj = _fold_lanes(s, jnp.maximum)
            mx = mj if mx is None else jnp.maximum(mx, mj)
            yield
        m = jnp.max(mx, axis=-1, keepdims=True)
        ls = None
        for j in range(c + 1):
            ks = slice(j * tq, (j + 1) * tq)
            p = jnp.exp2(s_ref[buf, rows, ks] - m)
            p_ref[buf, rows, ks] = p.astype(BF16)
            pj = _fold_lanes(p, jnp.add)
            ls = pj if ls is None else ls + pj
            yield
        l = jnp.sum(ls, axis=-1, keepdims=True)
        n = (c + 1) * tq
        acc = jnp.dot(p_ref[buf, rows, :n], va_ref[0, :n, :], preferred_element_type=F32)
        da_ref[hf] = acc / l

    later = jnp.where(jnp.concatenate([row, row], axis=0) > jnp.concatenate([col, col], axis=0),
                      1.0, 0.0).astype(BF16)
    strictly_before = col < row

    def sb_sweep(c, hf):
        buf = c % 2
        rows = slice(hf * tq, (hf + 1) * tq)
        q = (qc_ref[0, c * tq:(c + 1) * tq, :].astype(F32) * (SB_DIM ** -0.5 * log2e)).astype(BF16)
        qs = jnp.where(half_lanes[hf], q, jnp.zeros_like(q))
        carry = None
        for j in range(c, -1, -1):
            ks = slice(j * tq, (j + 1) * tq)
            z = lax.dot_general(qs, kc_ref[0, ks, :], _NT, preferred_element_type=F32)
            softplus = jnp.log(1.0 + jnp.exp2(-jnp.abs(z))) * log2e
            log_beta = jnp.minimum(z, 0.0) - softplus
            log_1m = log_beta - z
            if j == c:
                log_1m = jnp.where(strictly_before, log_1m, 0.0)
            hi = log_1m.astype(BF16)
            lo = (log_1m - hi.astype(F32)).astype(BF16)
            tail = jnp.dot(jnp.concatenate([hi, lo], axis=1), later,
                           preferred_element_type=F32)
            expo = log_beta + tail
            if carry is not None:
                expo = expo + carry
            a = jnp.exp2(expo)
            if j == c:
                a = jnp.where(strictly_before, a, 0.0)
            a_ref[buf, rows, ks] = a.astype(BF16)
            rs = jnp.sum(_fold_lanes(log_1m, jnp.add), axis=-1, keepdims=True)
            carry = rs if carry is None else carry + rs
            yield
        n = (c + 1) * tq
        sb_ref[hf] = jnp.dot(a_ref[buf, rows, :n], vc_ref[0, :n, :], preferred_element_type=F32)

    for c in range(nq):
        sweeps = [diff_sweep(c, 0), sb_sweep(c, 0), diff_sweep(c, 1), sb_sweep(c, 1)]
        while sweeps:
            for sweep in list(sweeps):
                if next(sweep, "done") == "done":
                    sweeps.remove(sweep)
        out_rows = slice(c * tq, (c + 1) * tq)
        o = da_ref[0] - lam * da_ref[1]
        oa_ref[0, out_rows, :] = (_rms(o, gsub) * (1.0 - lam_init)).astype(BF16)
        oc_ref[0, out_rows, :] = jnp.where(half_lanes[0], sb_ref[0], sb_ref[1]).astype(BF16)


def _mixers(u, rope, lam_p, gsub, *, tq, lam_init):
    b, s, _ = u.shape
    assert DA_HEADS == SB_WIDTH // LANES

    def slab(offset):
        return pl.BlockSpec((1, s, LANES), lambda bi, hi: (bi, 0, offset // LANES + hi))

    table = pl.BlockSpec((1, s, LANES), lambda bi, hi: (bi, 0, 0))
    return pl.pallas_call(
        functools.partial(_mixers_body, tq=tq, lam_init=lam_init),
        grid=(b, DA_HEADS),
        in_specs=[
            pl.BlockSpec((4, DA_QK), lambda bi, hi: (0, 0)),
            pl.BlockSpec((1, DA_V), lambda bi, hi: (0, 0)),
            table, table, table,
            slab(U_QA), slab(U_KA), slab(U_VA), slab(U_QC), slab(U_KC), slab(U_VC),
        ],
        out_specs=[slab(0), slab(0)],
        out_shape=[jax.ShapeDtypeStruct((b, s, DA_WIDTH), BF16),
                   jax.ShapeDtypeStruct((b, s, SB_WIDTH), BF16)],
        scratch_shapes=[pltpu.VMEM((s, LANES), BF16), pltpu.VMEM((s, LANES), BF16),
                        pltpu.VMEM((2, 2 * tq, s), F32), pltpu.VMEM((2, 2 * tq, s), BF16),
                        pltpu.VMEM((2, 2 * tq, s), BF16),
                        pltpu.VMEM((2, tq, DA_V), F32), pltpu.VMEM((2, tq, LANES), F32)],
        compiler_params=_params("parallel", "parallel"),
        name="mixers",
    )(lam_p, gsub, *rope, u, u, u, u, u, u)


def _gla_body(gate_ref, wup_ref, bup_ref, gnorm_ref, q_ref, k_ref, v_ref, og_ref, o_ref,
              loga_ref, state_ref):
    s = q_ref.shape[1]
    pre = jnp.dot(gate_ref[0].astype(BF16), wup_ref[...], preferred_element_type=F32)
    loga_ref[...] = _log_sigmoid(pre + bup_ref[...]) * (1.0 / GLA_GATE_NORM)
    state_ref[...] = jnp.zeros_like(state_ref)

    rr = lax.broadcasted_iota(jnp.int32, (CHUNK, CHUNK), 0)
    cc = lax.broadcasted_iota(jnp.int32, (CHUNK, CHUNK), 1)
    causal = rr >= cc
    prefix = jnp.where(causal, 1.0, 0.0).astype(BF16)
    gnorm = gnorm_ref[...]

    def chunk(n):
        sl = pl.ds(pl.multiple_of(n * CHUNK, CHUNK), CHUNK)
        la = loga_ref[sl, :]
        hi = la.astype(BF16)
        r1 = la - hi.astype(F32)
        mid = r1.astype(BF16)
        lo = (r1 - mid.astype(F32)).astype(BF16)
        bcum = (jnp.dot(prefix, hi, preferred_element_type=F32)
                + jnp.dot(prefix, mid, preferred_element_type=F32)
                + jnp.dot(prefix, lo, preferred_element_type=F32))
        b_last = bcum[CHUNK - 1:CHUNK, :]
        q = q_ref[0, sl, :].astype(F32) * GLA_DK ** -0.5
        k = k_ref[0, sl, :].astype(F32)
        q_d = (q * jnp.exp(bcum)).astype(BF16)
        k_d = (k * jnp.exp(-bcum)).astype(BF16)
        k_l = (k * jnp.exp(b_last - bcum)).astype(BF16)
        decay = jnp.exp(b_last)

        def head(hd):
            ks = slice(hd * GLA_DK, (hd + 1) * GLA_DK)
            vs = slice(hd * GLA_DV, (hd + 1) * GLA_DV)
            v = v_ref[0, sl, vs]
            att = lax.dot_general(q_d[:, ks], k_d[:, ks], _NT, preferred_element_type=F32)
            att = jnp.where(causal, att, 0.0).astype(BF16)
            yield
            st = state_ref[hd]
            o = (jnp.dot(att, v, preferred_element_type=F32)
                 + lax.dot_general(q_d[:, ks], st.astype(BF16), _NT, preferred_element_type=F32))
            yield
            state_ref[hd] = decay[:, ks] * st + lax.dot_general(
                v, k_l[:, ks], _TN, preferred_element_type=F32)
            yield
            og = og_ref[0, sl, vs].astype(F32)
            o_ref[0, sl, vs] = (_rms(o, gnorm) * (og * _sigmoid(og))).astype(BF16)

        heads = [head(hd) for hd in range(GLA_HEADS)]
        while heads:
            for h in list(heads):
                if next(h, "done") == "done":
                    heads.remove(h)

    group = math.gcd(GLA_CHUNKS_PER_TRIP, s // CHUNK)

    def trip(g, _):
        for t in range(group):
            chunk(g * group + t)
        return 0

    lax.fori_loop(0, s // CHUNK // group, trip, 0)


def _gla(u, gate, wup, bup, gnorm):
    b, s, _ = u.shape
    hk = GLA_HEADS * GLA_DK
    return pl.pallas_call(
        _gla_body,
        grid=(b,),
        in_specs=[
            pl.BlockSpec((1, s, LANES), lambda bi: (bi, 0, 0)),
            pl.BlockSpec((LANES, hk), lambda bi: (0, 0)),
            pl.BlockSpec((1, hk), lambda bi: (0, 0)),
            pl.BlockSpec((1, GLA_DV), lambda bi: (0, 0)),
            pl.BlockSpec((1, s, hk), lambda bi: (bi, 0, U_QB // hk)),
            pl.BlockSpec((1, s, hk), lambda bi: (bi, 0, U_KB // hk)),
            pl.BlockSpec((1, s, GLA_WIDTH), lambda bi: (bi, 0, U_VB // GLA_WIDTH)),
            pl.BlockSpec((1, s, GLA_WIDTH), lambda bi: (bi, 0, U_OB // GLA_WIDTH)),
        ],
        out_specs=pl.BlockSpec((1, s, GLA_WIDTH), lambda bi: (bi, 0, 0)),
        out_shape=jax.ShapeDtypeStruct((b, s, GLA_WIDTH), BF16),
        scratch_shapes=[pltpu.VMEM((s, hk), F32), pltpu.VMEM((GLA_HEADS, GLA_DV, GLA_DK), F32)],
        compiler_params=_params("parallel"),
        name="gla",
    )(gate, wup, bup, gnorm, u, u, u, u)


def _outproj_body(h_ref, oa_ref, ob_ref, oc_ref, w_ref, o_ref):
    b0, c0 = DA_WIDTH, DA_WIDTH + GLA_WIDTH
    o_ref[...] = (h_ref[...]
                  + jnp.dot(oa_ref[...], w_ref[:b0, :], preferred_element_type=F32)
                  + jnp.dot(ob_ref[...], w_ref[b0:c0, :], preferred_element_type=F32)
                  + jnp.dot(oc_ref[...], w_ref[c0:, :], preferred_element_type=F32))


def _outproj(h, oa, ob, oc, w, *, tm):
    m, d = h.shape

    def rows(width):
        return pl.BlockSpec((tm, width), lambda i: (i, 0))

    return pl.pallas_call(
        _outproj_body,
        grid=(m // tm,),
        in_specs=[rows(d), rows(DA_WIDTH), rows(GLA_WIDTH), rows(SB_WIDTH),
                  pl.BlockSpec(w.shape, lambda i: (0, 0))],
        out_specs=rows(d),
        out_shape=jax.ShapeDtypeStruct((m, d), F32),
        compiler_params=_params("parallel"),
        name="outproj",
    )(h, oa, ob, oc, w)


def _to_bf16_body(w_ref, o_ref):
    o_ref[...] = w_ref[0].astype(BF16)


def _layer_to_bf16(w, layer, *, br):
    _, r, c = w.shape
    assert r % br == 0
    return pl.pallas_call(
        _to_bf16_body,
        grid=(r // br,),
        in_specs=[pl.BlockSpec((1, br, c), lambda i: (layer, i, 0))],
        out_specs=pl.BlockSpec((br, c), lambda i: (i, 0)),
        out_shape=jax.ShapeDtypeStruct((r, c), BF16),
        compiler_params=_params("parallel"),
        name="to_bf16",
    )(w)


def _ple_body(h_ref, p_ref, g_ref, wg_ref, wp_ref, gf_ref, o_ref, *, final):
    x = h_ref[...]
    n = _rms(x, g_ref[...]).astype(BF16)
    gate = _sigmoid(jnp.dot(n, wg_ref[...], preferred_element_type=F32))
    emb = jnp.dot(p_ref[...].astype(BF16), wp_ref[...], preferred_element_type=F32)
    y = x + emb * gate
    o_ref[...] = _rms(y, gf_ref[...]) if final else y


def _ple(h, p, g, wg, wp, gf, *, tm, final):
    m, d = h.shape
    pd = p.shape[1]
    return pl.pallas_call(
        functools.partial(_ple_body, final=final),
        grid=(m // tm,),
        in_specs=[
            pl.BlockSpec((tm, d), lambda i: (i, 0)),
            pl.BlockSpec((tm, pd), lambda i: (i, 0)),
            pl.BlockSpec((1, d), lambda i: (0, 0)),
            pl.BlockSpec((d, d), lambda i: (0, 0)),
            pl.BlockSpec((pd, d), lambda i: (0, 0)),
            pl.BlockSpec((1, d), lambda i: (0, 0)),
        ],
        out_specs=pl.BlockSpec((tm, d), lambda i: (i, 0)),
        out_shape=jax.ShapeDtypeStruct((m, d), F32),
        compiler_params=_params("parallel"),
        name="ple",
    )(h, p, g, wg, wp, gf)


def _reorder_w_in(w):
    sizes = (DA_WIDTH, DA_WIDTH, DA_WIDTH, GLA_HEADS * GLA_DK, GLA_HEADS * GLA_DK, GLA_WIDTH,
             GLA_RANK, GLA_WIDTH, SB_WIDTH, SB_WIDTH, SB_WIDTH)
    parts, start = [], 0
    for size in sizes:
        parts.append(w[:, start:start + size])
        start += size
    qa, ka, va, qb, kb, vb, gb, ob, qc, kc, vc = parts
    main = jnp.concatenate([qa, ka, va, qb, vb, ob, kb, qc, kc, vc], axis=1).astype(BF16)
    gate = jnp.pad(gb, ((0, 0), (0, LANES - GLA_RANK))).astype(BF16)
    return main, gate


def kernel(x, p, positions, g_ffn1, w_ffn1_gate, w_ffn1_up, w_ffn1_down, g_mix, w_in, w_gla_gate_up, b_gla_gate, g_gla_norm, lambda_q1, lambda_k1, lambda_q2, lambda_k2, g_diff_norm, w_out, g_ffn2, w_ffn2_gate, w_ffn2_up, w_ffn2_down, g_ple, w_ple_gate, w_ple_proj, g_final):
    b, s, d = x.shape
    m = b * s
    tm = min(512, m)
    tm_big = min(1024, m)
    tq = min(256, s)
    assert m % tm_big == 0 and m % tm == 0 and s % tq == 0 and tq % CHUNK == 0

    inv_freq = ROPE_THETA ** (-jnp.arange(0, ROPE_DIMS, 2, dtype=F32) / ROPE_DIMS)
    freq_lanes = jnp.tile(inv_freq, LANES // inv_freq.shape[0]).reshape(1, LANES)
    pos_lanes = jnp.broadcast_to(positions.astype(F32).reshape(m, 1), (m, LANES))
    rope_c, rope_s1, rope_s2 = _rope_tables(pos_lanes, freq_lanes, tm=tm)
    rope = [t.reshape(b, s, LANES) for t in (rope_c, rope_s1, rope_s2)]

    def ffn_weights(wg, wu, wd, i):
        return (_layer_to_bf16(wg, i, br=256), _layer_to_bf16(wu, i, br=256),
                _layer_to_bf16(wd, i, br=512))

    h = x.reshape(m, d)
    for i in range(DEPTH):
        row = lambda a: a[i].reshape(1, -1)
        h = _ffn(h, row(g_ffn1), *ffn_weights(w_ffn1_gate, w_ffn1_up, w_ffn1_down, i),
                 tm=tm_big, tf=512)

        w_main, w_gate = _reorder_w_in(w_in[i])
        u, gate_low = _inproj(h, row(g_mix), w_main, w_gate, tm=tm_big, tn=2048)
        u3 = u.reshape(b, s, U_WIDTH)

        lam_init = 0.8 - 0.6 * math.exp(-0.3 * i)
        lam_p = jnp.stack([lambda_q1[i], lambda_k1[i], lambda_q2[i], lambda_k2[i]])
        o_a, o_c = _mixers(u3, rope, lam_p, row(g_diff_norm), tq=tq, lam_init=lam_init)
        wup = jnp.pad(w_gla_gate_up[i], ((0, LANES - GLA_RANK), (0, 0))).astype(BF16)
        o_b = _gla(u3, gate_low.reshape(b, s, LANES), wup, row(b_gla_gate), row(g_gla_norm))

        h = _outproj(h, o_a.reshape(m, DA_WIDTH), o_b.reshape(m, GLA_WIDTH),
                     o_c.reshape(m, SB_WIDTH), _layer_to_bf16(w_out, i, br=512), tm=tm)

        h = _ffn(h, row(g_ffn2), *ffn_weights(w_ffn2_gate, w_ffn2_up, w_ffn2_down, i),
                 tm=tm_big, tf=512)
        h = _ple(h, p[i].reshape(m, PLE_DIM), row(g_ple), _layer_to_bf16(w_ple_gate, i, br=512),
                 w_ple_proj[i].astype(BF16), g_final.reshape(1, -1), tm=tm,
                 final=(i == DEPTH - 1))
    return h.reshape(b, s, d)
```

```python
import functools
import math

import jax
import jax.numpy as jnp
from jax import lax
from jax.experimental import pallas as pl
from jax.experimental.pallas import tpu as pltpu

F32 = jnp.float32
BF16 = jnp.bfloat16

D_MODEL = 2048
DEPTH = 2
CHUNK = 64
DA_HEADS = 4
DA_QK = 64
DA_V = 128
DA_WIDTH = 512
GLA_HEADS = 4
GLA_DK = 128
GLA_DV = 256
GLA_RANK = 16
GLA_GATE_NORM = 16.0
GLA_WIDTH = 1024
SB_HEADS = 8
SB_DIM = 64
SB_WIDTH = 512
ROPE_THETA = 500000.0
ROPE_DIMS = 16
D_FF = 5632
PLE_DIM = 256
EPS = 1e-6

LANES = 128
U_QA, U_KA, U_VA, U_QB, U_VB, U_OB, U_KB, U_QC, U_KC, U_VC = (
    0, 512, 1024, 1536, 2048, 3072, 4096, 4608, 5120, 5632)
U_WIDTH = 6144

VMEM_LIMIT = 56 * 1024 * 1024
GLA_CHUNKS_PER_TRIP = 4

_NT = (((1,), (1,)), ((), ()))
_TN = (((0,), (0,)), ((), ()))


def _params(*sem):
    return pltpu.CompilerParams(dimension_semantics=sem, vmem_limit_bytes=VMEM_LIMIT)


def _rms(x, g):
    return x * lax.rsqrt(jnp.mean(x * x, axis=-1, keepdims=True) + EPS) * g


def _log_sigmoid(z):
    return jnp.minimum(z, 0.0) - jnp.log(1.0 + jnp.exp(-jnp.abs(z)))


def _sigmoid(z):
    return 1.0 / (1.0 + jnp.exp(-z))


def _ffn_body(h_ref, g_ref, wg_ref, wu_ref, wd_ref, o_ref, n_ref):
    @pl.when(pl.program_id(1) == 0)
    def _():
        x = h_ref[...]
        n_ref[...] = _rms(x, g_ref[...]).astype(BF16)
        o_ref[...] = x

    n = n_ref[...]
    a = jnp.dot(n, wg_ref[...], preferred_element_type=F32)
    b = jnp.dot(n, wu_ref[...], preferred_element_type=F32)
    act = (0.5 * a * _sigmoid(a) * b).astype(BF16)
    o_ref[...] += jnp.dot(act, wd_ref[...], preferred_element_type=F32)


def _ffn(h, g, wg, wu, wd, *, tm, tf):
    m, d = h.shape
    f = wg.shape[1]
    return pl.pallas_call(
        _ffn_body,
        grid=(m // tm, f // tf),
        in_specs=[
            pl.BlockSpec((tm, d), lambda i, j: (i, 0)),
            pl.BlockSpec((1, d), lambda i, j: (0, 0)),
            pl.BlockSpec((d, tf), lambda i, j: (0, j)),
            pl.BlockSpec((d, tf), lambda i, j: (0, j)),
            pl.BlockSpec((tf, d), lambda i, j: (j, 0)),
        ],
        out_specs=pl.BlockSpec((tm, d), lambda i, j: (i, 0)),
        out_shape=jax.ShapeDtypeStruct((m, d), F32),
        scratch_shapes=[pltpu.VMEM((tm, d), BF16)],
        compiler_params=_params("parallel", "arbitrary"),
        name="ffn",
    )(h, g, wg, wu, wd)


def _inproj_body(h_ref, g_ref, w_ref, wgate_ref, u_ref, gate_ref, n_ref):
    @pl.when(pl.program_id(1) == 0)
    def _():
        n = _rms(h_ref[...], g_ref[...]).astype(BF16)
        n_ref[...] = n
        gate_ref[...] = jnp.dot(n, wgate_ref[...], preferred_element_type=F32)

    u_ref[...] = jnp.dot(n_ref[...], w_ref[...], preferred_element_type=F32).astype(BF16)


def _inproj(h, g, w, wgate, *, tm, tn):
    m, d = h.shape
    n = w.shape[1]
    return pl.pallas_call(
        _inproj_body,
        grid=(m // tm, n // tn),
        in_specs=[
            pl.BlockSpec((tm, d), lambda i, j: (i, 0)),
            pl.BlockSpec((1, d), lambda i, j: (0, 0)),
            pl.BlockSpec((d, tn), lambda i, j: (0, j)),
            pl.BlockSpec((d, LANES), lambda i, j: (0, 0)),
        ],
        out_specs=[
            pl.BlockSpec((tm, tn), lambda i, j: (i, j)),
            pl.BlockSpec((tm, LANES), lambda i, j: (i, 0)),
        ],
        out_shape=[
            jax.ShapeDtypeStruct((m, n), BF16),
            jax.ShapeDtypeStruct((m, LANES), F32),
        ],
        scratch_shapes=[pltpu.VMEM((tm, d), BF16)],
        compiler_params=_params("parallel", "arbitrary"),
        name="inproj",
    )(h, g, w, wgate)


def _rope_table_body(pos_ref, freq_ref, c_ref, s1_ref, s2_ref):
    ang = pos_ref[...] * freq_ref[...]
    c = jnp.cos(ang)
    s = jnp.sin(ang)
    dim = lax.broadcasted_iota(jnp.int32, ang.shape, 1) % DA_QK
    half = ROPE_DIMS // 2
    c_ref[...] = jnp.where(dim < ROPE_DIMS, c, 1.0)
    s1_ref[...] = jnp.where(dim < half, -s, 0.0)
    s2_ref[...] = jnp.where((dim >= half) & (dim < ROPE_DIMS), s, 0.0)


def _rope_tables(pos_lanes, freq_lanes, *, tm):
    m = pos_lanes.shape[0]
    spec = pl.BlockSpec((tm, LANES), lambda i: (i, 0))
    return pl.pallas_call(
        _rope_table_body,
        grid=(m // tm,),
        in_specs=[spec, pl.BlockSpec((1, LANES), lambda i: (0, 0))],
        out_specs=[spec, spec, spec],
        out_shape=[jax.ShapeDtypeStruct((m, LANES), F32)] * 3,
        compiler_params=_params("parallel"),
        name="rope_tables",
    )(pos_lanes, freq_lanes)


def _fold_lanes(x, op):
    out = x[:, :LANES]
    for g in range(1, x.shape[1] // LANES):
        out = op(out, x[:, g * LANES:(g + 1) * LANES])
    return out


def _mixers_body(lam_ref, gsub_ref, cos_ref, sin_lo_ref, sin_hi_ref, qa_ref, ka_ref, va_ref,
                 qc_ref, kc_ref, vc_ref, oa_ref, oc_ref, qr_ref, kr_ref, s_ref, p_ref, a_ref,
                 da_ref, sb_ref, *, tq, lam_init):
    nq = qa_ref.shape[1] // tq
    log2e = math.log2(math.e)
    row = lax.broadcasted_iota(jnp.int32, (tq, tq), 0)
    col = lax.broadcasted_iota(jnp.int32, (tq, tq), 1)
    lane = lax.broadcasted_iota(jnp.int32, (tq, LANES), 1)
    half_lanes = (lane < LANES // 2, lane >= LANES // 2)

    half = ROPE_DIMS // 2
    cos, sin_lo, sin_hi = cos_ref[0], sin_lo_ref[0], sin_hi_ref[0]

    def rot(x):
        return (x * cos + pltpu.roll(x, LANES - half, 1) * sin_lo
                + pltpu.roll(x, half, 1) * sin_hi)

    qr_ref[...] = (rot(qa_ref[0].astype(F32)) * (DA_QK ** -0.5 * log2e)).astype(BF16)
    kr_ref[...] = rot(ka_ref[0].astype(F32)).astype(BF16)
    lam_p = lam_ref[...]
    lam = (jnp.exp(jnp.sum(lam_p[0:1] * lam_p[1:2], axis=-1, keepdims=True))
           - jnp.exp(jnp.sum(lam_p[2:3] * lam_p[3:4], axis=-1, keepdims=True)) + lam_init)
    gsub = gsub_ref[...]
    chunk_mask = col // CHUNK <= row // CHUNK

    def diff_sweep(c, hf):
        buf = c % 2
        rows = slice(hf * tq, (hf + 1) * tq)
        q = qr_ref[c * tq:(c + 1) * tq, :]
        qs = jnp.where(half_lanes[hf], q, jnp.zeros_like(q))
        mx = None
        for j in range(c + 1):
            ks = slice(j * tq, (j + 1) * tq)
            s = lax.dot_general(qs, kr_ref[ks, :], _NT, preferred_element_type=F32)
            if j == c:
                s = jnp.where(chunk_mask, s, -jnp.inf)
            s_ref[buf, rows, ks] = s
            mj = _fold_lanes(s, jnp.maximum)
            mx = mj if mx is None else jnp.maximum(mx, mj)
            yield
        m = jnp.max(mx, axis=-1, keepdims=True)
        ls = None
        for j in range(c + 1):
            ks = slice(j * tq, (j + 1) * tq)
            p = jnp.exp2(s_ref[buf, rows, ks] - m)
            p_ref[buf, rows, ks] = p.astype(BF16)
            pj = _fold_lanes(p, jnp.add)
            ls = pj if ls is None else ls + pj
            yield
        l = jnp.sum(ls, axis=-1, keepdims=True)
        n = (c + 1) * tq
        acc = jnp.dot(p_ref[buf, rows, :n], va_ref[0, :n, :], preferred_element_type=F32)
        da_ref[hf] = acc / l

    later = jnp.where(jnp.concatenate([row, row], axis=0) > jnp.concatenate([col, col], axis=0),
                      1.0, 0.0).astype(BF16)
    strictly_before = col < row

    def sb_sweep(c, hf):
        buf = c % 2
        rows = slice(hf * tq, (hf + 1) * tq)
        q = (qc_ref[0, c * tq:(c + 1) * tq, :].astype(F32) * (SB_DIM ** -0.5 * log2e)).astype(BF16)
        qs = jnp.where(half_lanes[hf], q, jnp.zeros_like(q))
        carry = None
        for j in range(c, -1, -1):
            ks = slice(j * tq, (j + 1) * tq)
            z = lax.dot_general(qs, kc_ref[0, ks, :], _NT, preferred_element_type=F32)
            softplus = jnp.log(1.0 + jnp.exp2(-jnp.abs(z))) * log2e
            log_beta = jnp.minimum(z, 0.0) - softplus
            log_1m = log_beta - z
            if j == c:
                log_1m = jnp.where(strictly_before, log_1m, 0.0)
            hi = log_1m.astype(BF16)
            lo = (log_1m - hi.astype(F32)).astype(BF16)
            tail = jnp.dot(jnp.concatenate([hi, lo], axis=1), later,
                           preferred_element_type=F32)
            expo = log_beta + tail
            if carry is not None:
                expo = expo + carry
            a = jnp.exp2(expo)
            if j == c:
                a = jnp.where(strictly_before, a, 0.0)
            a_ref[buf, rows, ks] = a.astype(BF16)
            rs = jnp.sum(_fold_lanes(log_1m, jnp.add), axis=-1, keepdims=True)
            carry = rs if carry is None else carry + rs
            yield
        n = (c + 1) * tq
        sb_ref[hf] = jnp.dot(a_ref[buf, rows, :n], vc_ref[0, :n, :], preferred_element_type=F32)

    for c in range(nq):
        sweeps = [diff_sweep(c, 0), sb_sweep(c, 0), diff_sweep(c, 1), sb_sweep(c, 1)]
        while sweeps:
            for sweep in list(sweeps):
                if next(sweep, "done") == "done":
                    sweeps.remove(sweep)
        out_rows = slice(c * tq, (c + 1) * tq)
        o = da_ref[0] - lam * da_ref[1]
        oa_ref[0, out_rows, :] = (_rms(o, gsub) * (1.0 - lam_init)).astype(BF16)
        oc_ref[0, out_rows, :] = jnp.where(half_lanes[0], sb_ref[0], sb_ref[1]).astype(BF16)


def _mixers(u, rope, lam_p, gsub, *, tq, lam_init):
    b, s, _ = u.shape
    assert DA_HEADS == SB_WIDTH // LANES

    def slab(offset):
        return pl.BlockSpec((1, s, LANES), lambda bi, hi: (bi, 0, offset // LANES + hi))

    table = pl.BlockSpec((1, s, LANES), lambda bi, hi: (bi, 0, 0))
    return pl.pallas_call(
        functools.partial(_mixers_body, tq=tq, lam_init=lam_init),
        grid=(b, DA_HEADS),
        in_specs=[
            pl.BlockSpec((4, DA_QK), lambda bi, hi: (0, 0)),
            pl.BlockSpec((1, DA_V), lambda bi, hi: (0, 0)),
            table, table, table,
            slab(U_QA), slab(U_KA), slab(U_VA), slab(U_QC), slab(U_KC), slab(U_VC),
        ],
        out_specs=[slab(0), slab(0)],
        out_shape=[jax.ShapeDtypeStruct((b, s, DA_WIDTH), BF16),
                   jax.ShapeDtypeStruct((b, s, SB_WIDTH), BF16)],
        scratch_shapes=[pltpu.VMEM((s, LANES), BF16), pltpu.VMEM((s, LANES), BF16),
                        pltpu.VMEM((2, 2 * tq, s), F32), pltpu.VMEM((2, 2 * tq, s), BF16),
                        pltpu.VMEM((2, 2 * tq, s), BF16),
                        pltpu.VMEM((2, tq, DA_V), F32), pltpu.VMEM((2, tq, LANES), F32)],
        compiler_params=_params("parallel", "parallel"),
        name="mixers",
    )(lam_p, gsub, *rope, u, u, u, u, u, u)


def _gla_body(gate_ref, wup_ref, bup_ref, gnorm_ref, q_ref, k_ref, v_ref, og_ref, o_ref,
              loga_ref, state_ref):
    s = q_ref.shape[1]
    pre = jnp.dot(gate_ref[0].astype(BF16), wup_ref[...], preferred_element_type=F32)
    loga_ref[...] = _log_sigmoid(pre + bup_ref[...]) * (1.0 / GLA_GATE_NORM)
    state_ref[...] = jnp.zeros_like(state_ref)

    rr = lax.broadcasted_iota(jnp.int32, (CHUNK, CHUNK), 0)
    cc = lax.broadcasted_iota(jnp.int32, (CHUNK, CHUNK), 1)
    causal = rr >= cc
    prefix = jnp.where(causal, 1.0, 0.0).astype(BF16)
    gnorm = gnorm_ref[...]

    def chunk(n):
        sl = pl.ds(pl.multiple_of(n * CHUNK, CHUNK), CHUNK)
        la = loga_ref[sl, :]
        hi = la.astype(BF16)
        r1 = la - hi.astype(F32)
        mid = r1.astype(BF16)
        lo = (r1 - mid.astype(F32)).astype(BF16)
        bcum = (jnp.dot(prefix, hi, preferred_element_type=F32)
                + jnp.dot(prefix, mid, preferred_element_type=F32)
                + jnp.dot(prefix, lo, preferred_element_type=F32))
        b_last = bcum[CHUNK - 1:CHUNK, :]
        q = q_ref[0, sl, :].astype(F32) * GLA_DK ** -0.5
        k = k_ref[0, sl, :].astype(F32)
        q_d = (q * jnp.exp(bcum)).astype(BF16)
        k_d = (k * jnp.exp(-bcum)).astype(BF16)
        k_l = (k * jnp.exp(b_last - bcum)).astype(BF16)
        decay = jnp.exp(b_last)
        yield

        def head(hd):
            ks = slice(hd * GLA_DK, (hd + 1) * GLA_DK)
            vs = slice(hd * GLA_DV, (hd + 1) * GLA_DV)
            v = v_ref[0, sl, vs]
            att = lax.dot_general(q_d[:, ks], k_d[:, ks], _NT, preferred_element_type=F32)
            att = jnp.where(causal, att, 0.0).astype(BF16)
            yield
            st = state_ref[hd]
            o = (jnp.dot(att, v, preferred_element_type=F32)
                 + lax.dot_general(q_d[:, ks], st.astype(BF16), _NT, preferred_element_type=F32))
            yield
            state_ref[hd] = decay[:, ks] * st + lax.dot_general(
                v, k_l[:, ks], _TN, preferred_element_type=F32)
            yield
            og = og_ref[0, sl, vs].astype(F32)
            o_ref[0, sl, vs] = (_rms(o, gnorm) * (og * _sigmoid(og))).astype(BF16)

        heads = [head(hd) for hd in range(GLA_HEADS)]
        while heads:
            for h in list(heads):
                if next(h, "done") == "done":
                    heads.remove(h)

    group = math.gcd(GLA_CHUNKS_PER_TRIP, s // CHUNK)

    def trip(g, _):
        chunks = [chunk(g * group + t) for t in range(group)]
        for ch in chunks:
            next(ch)
        for ch in chunks:
            for _ in ch:
                pass
        return 0

    lax.fori_loop(0, s // CHUNK // group, trip, 0)


def _gla(u, gate, wup, bup, gnorm):
    b, s, _ = u.shape
    hk = GLA_HEADS * GLA_DK
    return pl.pallas_call(
        _gla_body,
        grid=(b,),
        in_specs=[
            pl.BlockSpec((1, s, LANES), lambda bi: (bi, 0, 0)),
            pl.BlockSpec((LANES, hk), lambda bi: (0, 0)),
            pl.BlockSpec((1, hk), lambda bi: (0, 0)),
            pl.BlockSpec((1, GLA_DV), lambda bi: (0, 0)),
            pl.BlockSpec((1, s, hk), lambda bi: (bi, 0, U_QB // hk)),
            pl.BlockSpec((1, s, hk), lambda bi: (bi, 0, U_KB // hk)),
            pl.BlockSpec((1, s, GLA_WIDTH), lambda bi: (bi, 0, U_VB // GLA_WIDTH)),
            pl.BlockSpec((1, s, GLA_WIDTH), lambda bi: (bi, 0, U_OB // GLA_WIDTH)),
        ],
        out_specs=pl.BlockSpec((1, s, GLA_WIDTH), lambda bi: (bi, 0, 0)),
        out_shape=jax.ShapeDtypeStruct((b, s, GLA_WIDTH), BF16),
        scratch_shapes=[pltpu.VMEM((s, hk), F32), pltpu.VMEM((GLA_HEADS, GLA_DV, GLA_DK), F32)],
        compiler_params=_params("parallel"),
        name="gla",
    )(gate, wup, bup, gnorm, u, u, u, u)


def _outproj_body(h_ref, oa_ref, ob_ref, oc_ref, w_ref, o_ref):
    b0, c0 = DA_WIDTH, DA_WIDTH + GLA_WIDTH
    o_ref[...] = (h_ref[...]
                  + jnp.dot(oa_ref[...], w_ref[:b0, :], preferred_element_type=F32)
                  + jnp.dot(ob_ref[...], w_ref[b0:c0, :], preferred_element_type=F32)
                  + jnp.dot(oc_ref[...], w_ref[c0:, :], preferred_element_type=F32))


def _outproj(h, oa, ob, oc, w, *, tm):
    m, d = h.shape

    def rows(width):
        return pl.BlockSpec((tm, width), lambda i: (i, 0))

    return pl.pallas_call(
        _outproj_body,
        grid=(m // tm,),
        in_specs=[rows(d), rows(DA_WIDTH), rows(GLA_WIDTH), rows(SB_WIDTH),
                  pl.BlockSpec(w.shape, lambda i: (0, 0))],
        out_specs=rows(d),
        out_shape=jax.ShapeDtypeStruct((m, d), F32),
        compiler_params=_params("parallel"),
        name="outproj",
    )(h, oa, ob, oc, w)


def _to_bf16_body(w_ref, o_ref):
    o_ref[...] = w_ref[0].astype(BF16)


def _layer_to_bf16(w, layer, *, br):
    _, r, c = w.shape
    assert r % br == 0
    return pl.pallas_call(
        _to_bf16_body,
        grid=(r // br,),
        in_specs=[pl.BlockSpec((1, br, c), lambda i: (layer, i, 0))],
        out_specs=pl.BlockSpec((br, c), lambda i: (i, 0)),
        out_shape=jax.ShapeDtypeStruct((r, c), BF16),
        compiler_params=_params("parallel"),
        name="to_bf16",
    )(w)


def _ple_body(h_ref, p_ref, g_ref, wg_ref, wp_ref, gf_ref, o_ref, *, final):
    x = h_ref[...]
    n = _rms(x, g_ref[...]).astype(BF16)
    gate = _sigmoid(jnp.dot(n, wg_ref[...], preferred_element_type=F32))
    emb = jnp.dot(p_ref[...].astype(BF16), wp_ref[...], preferred_element_type=F32)
    y = x + emb * gate
    o_ref[...] = _rms(y, gf_ref[...]) if final else y


def _ple(h, p, g, wg, wp, gf, *, tm, final):
    m, d = h.shape
    pd = p.shape[1]
    return pl.pallas_call(
        functools.partial(_ple_body, final=final),
        grid=(m // tm,),
        in_specs=[
            pl.BlockSpec((tm, d), lambda i: (i, 0)),
            pl.BlockSpec((tm, pd), lambda i: (i, 0)),
            pl.BlockSpec((1, d), lambda i: (0, 0)),
            pl.BlockSpec((d, d), lambda i: (0, 0)),
            pl.BlockSpec((pd, d), lambda i: (0, 0)),
            pl.BlockSpec((1, d), lambda i: (0, 0)),
        ],
        out_specs=pl.BlockSpec((tm, d), lambda i: (i, 0)),
        out_shape=jax.ShapeDtypeStruct((m, d), F32),
        compiler_params=_params("parallel"),
        name="ple",
    )(h, p, g, wg, wp, gf)


def _reorder_w_in(w):
    sizes = (DA_WIDTH, DA_WIDTH, DA_WIDTH, GLA_HEADS * GLA_DK, GLA_HEADS * GLA_DK, GLA_WIDTH,
             GLA_RANK, GLA_WIDTH, SB_WIDTH, SB_WIDTH, SB_WIDTH)
    parts, start = [], 0
    for size in sizes:
        parts.append(w[:, start:start + size])
        start += size
    qa, ka, va, qb, kb, vb, gb, ob, qc, kc, vc = parts
    main = jnp.concatenate([qa, ka, va, qb, vb, ob, kb, qc, kc, vc], axis=1).astype(BF16)
    gate = jnp.pad(gb, ((0, 0), (0, LANES - GLA_RANK))).astype(BF16)
    return main, gate


def kernel(x, p, positions, g_ffn1, w_ffn1_gate, w_ffn1_up, w_ffn1_down, g_mix, w_in, w_gla_gate_up, b_gla_gate, g_gla_norm, lambda_q1, lambda_k1, lambda_q2, lambda_k2, g_diff_norm, w_out, g_ffn2, w_ffn2_gate, w_ffn2_up, w_ffn2_down, g_ple, w_ple_gate, w_ple_proj, g_final):
    b, s, d = x.shape
    m = b * s
    tm = min(512, m)
    tm_big = min(1024, m)
    tq = min(256, s)
    assert m % tm_big == 0 and m % tm == 0 and s % tq == 0 and tq % CHUNK == 0

    inv_freq = ROPE_THETA ** (-jnp.arange(0, ROPE_DIMS, 2, dtype=F32) / ROPE_DIMS)
    freq_lanes = jnp.tile(inv_freq, LANES // inv_freq.shape[0]).reshape(1, LANES)
    pos_lanes = jnp.broadcast_to(positions.astype(F32).reshape(m, 1), (m, LANES))
    rope_c, rope_s1, rope_s2 = _rope_tables(pos_lanes, freq_lanes, tm=tm)
    rope = [t.reshape(b, s, LANES) for t in (rope_c, rope_s1, rope_s2)]

    def ffn_weights(wg, wu, wd, i):
        return (_layer_to_bf16(wg, i, br=256), _layer_to_bf16(wu, i, br=256),
                _layer_to_bf16(wd, i, br=512))

    h = x.reshape(m, d)
    for i in range(DEPTH):
        row = lambda a: a[i].reshape(1, -1)
        h = _ffn(h, row(g_ffn1), *ffn_weights(w_ffn1_gate, w_ffn1_up, w_ffn1_down, i),
                 tm=tm_big, tf=512)

        w_main, w_gate = _reorder_w_in(w_in[i])
        u, gate_low = _inproj(h, row(g_mix), w_main, w_gate, tm=tm_big, tn=2048)
        u3 = u.reshape(b, s, U_WIDTH)

        lam_init = 0.8 - 0.6 * math.exp(-0.3 * i)
        lam_p = jnp.stack([lambda_q1[i], lambda_k1[i], lambda_q2[i], lambda_k2[i]])
        o_a, o_c = _mixers(u3, rope, lam_p, row(g_diff_norm), tq=tq, lam_init=lam_init)
        wup = jnp.pad(w_gla_gate_up[i], ((0, LANES - GLA_RANK), (0, 0))).astype(BF16)
        o_b = _gla(u3, gate_low.reshape(b, s, LANES), wup, row(b_gla_gate), row(g_gla_norm))

        h = _outproj(h, o_a.reshape(m, DA_WIDTH), o_b.reshape(m, GLA_WIDTH),
                     o_c.reshape(m, SB_WIDTH), _layer_to_bf16(w_out, i, br=512), tm=tm)

        h = _ffn(h, row(g_ffn2), *ffn_weights(w_ffn2_gate, w_ffn2_up, w_ffn2_down, i),
                 tm=tm_big, tf=512)
        h = _ple(h, p[i].reshape(m, PLE_DIM), row(g_ple), _layer_to_bf16(w_ple_gate, i, br=512),
                 w_ple_proj[i].astype(BF16), g_final.reshape(1, -1), tm=tm,
                 final=(i == DEPTH - 1))
    return h.reshape(b, s, d)
```

```python
import functools
import math

import jax
import jax.numpy as jnp
from jax import lax
from jax.experimental import pallas as pl
from jax.experimental.pallas import tpu as pltpu

F32 = jnp.float32
BF16 = jnp.bfloat16

D_MODEL = 2048
DEPTH = 2
CHUNK = 64
DA_HEADS = 4
DA_QK = 64
DA_V = 128
DA_WIDTH = 512
GLA_HEADS = 4
GLA_DK = 128
GLA_DV = 256
GLA_RANK = 16
GLA_GATE_NORM = 16.0
GLA_WIDTH = 1024
SB_HEADS = 8
SB_DIM = 64
SB_WIDTH = 512
ROPE_THETA = 500000.0
ROPE_DIMS = 16
D_FF = 5632
PLE_DIM = 256
EPS = 1e-6

LANES = 128
U_QA, U_KA, U_VA, U_QB, U_VB, U_OB, U_KB, U_QC, U_KC, U_VC = (
    0, 512, 1024, 1536, 2048, 3072, 4096, 4608, 5120, 5632)
U_WIDTH = 6144

VMEM_LIMIT = 56 * 1024 * 1024
GLA_CHUNKS_PER_TRIP = 8

_NT = (((1,), (1,)), ((), ()))
_TN = (((0,), (0,)), ((), ()))


def _params(*sem):
    return pltpu.CompilerParams(dimension_semantics=sem, vmem_limit_bytes=VMEM_LIMIT)


def _rms(x, g):
    return x * lax.rsqrt(jnp.mean(x * x, axis=-1, keepdims=True) + EPS) * g


def _log_sigmoid(z):
    return jnp.minimum(z, 0.0) - jnp.log(1.0 + jnp.exp(-jnp.abs(z)))


def _sigmoid(z):
    return 1.0 / (1.0 + jnp.exp(-z))


def _ffn_body(h_ref, g_ref, wg_ref, wu_ref, wd_ref, o_ref, n_ref):
    @pl.when(pl.program_id(1) == 0)
    def _():
        x = h_ref[...]
        n_ref[...] = _rms(x, g_ref[...]).astype(BF16)
        o_ref[...] = x

    n = n_ref[...]
    a = jnp.dot(n, wg_ref[...], preferred_element_type=F32)
    b = jnp.dot(n, wu_ref[...], preferred_element_type=F32)
    act = (0.5 * a * _sigmoid(a) * b).astype(BF16)
    o_ref[...] += jnp.dot(act, wd_ref[...], preferred_element_type=F32)


def _ffn(h, g, wg, wu, wd, *, tm, tf):
    m, d = h.shape
    f = wg.shape[1]
    return pl.pallas_call(
        _ffn_body,
        grid=(m // tm, f // tf),
        in_specs=[
            pl.BlockSpec((tm, d), lambda i, j: (i, 0)),
            pl.BlockSpec((1, d), lambda i, j: (0, 0)),
            pl.BlockSpec((d, tf), lambda i, j: (0, j)),
            pl.BlockSpec((d, tf), lambda i, j: (0, j)),
            pl.BlockSpec((tf, d), lambda i, j: (j, 0)),
        ],
        out_specs=pl.BlockSpec((tm, d), lambda i, j: (i, 0)),
        out_shape=jax.ShapeDtypeStruct((m, d), F32),
        scratch_shapes=[pltpu.VMEM((tm, d), BF16)],
        compiler_params=_params("parallel", "arbitrary"),
        name="ffn",
    )(h, g, wg, wu, wd)


def _inproj_body(h_ref, g_ref, w_ref, wgate_ref, u_ref, gate_ref, n_ref):
    @pl.when(pl.program_id(1) == 0)
    def _():
        n = _rms(h_ref[...], g_ref[...]).astype(BF16)
        n_ref[...] = n
        gate_ref[...] = jnp.dot(n, wgate_ref[...], preferred_element_type=F32)

    u_ref[...] = jnp.dot(n_ref[...], w_ref[...], preferred_element_type=F32).astype(BF16)


def _inproj(h, g, w, wgate, *, tm, tn):
    m, d = h.shape
    n = w.shape[1]
    return pl.pallas_call(
        _inproj_body,
        grid=(m // tm, n // tn),
        in_specs=[
            pl.BlockSpec((tm, d), lambda i, j: (i, 0)),
            pl.BlockSpec((1, d), lambda i, j: (0, 0)),
            pl.BlockSpec((d, tn), lambda i, j: (0, j)),
            pl.BlockSpec((d, LANES), lambda i, j: (0, 0)),
        ],
        out_specs=[
            pl.BlockSpec((tm, tn), lambda i, j: (i, j)),
            pl.BlockSpec((tm, LANES), lambda i, j: (i, 0)),
        ],
        out_shape=[
            jax.ShapeDtypeStruct((m, n), BF16),
            jax.ShapeDtypeStruct((m, LANES), F32),
        ],
        scratch_shapes=[pltpu.VMEM((tm, d), BF16)],
        compiler_params=_params("parallel", "arbitrary"),
        name="inproj",
    )(h, g, w, wgate)


def _rope_table_body(pos_ref, freq_ref, c_ref, s1_ref, s2_ref):
    ang = pos_ref[...] * freq_ref[...]
    c = jnp.cos(ang)
    s = jnp.sin(ang)
    dim = lax.broadcasted_iota(jnp.int32, ang.shape, 1) % DA_QK
    half = ROPE_DIMS // 2
    c_ref[...] = jnp.where(dim < ROPE_DIMS, c, 1.0)
    s1_ref[...] = jnp.where(dim < half, -s, 0.0)
    s2_ref[...] = jnp.where((dim >= half) & (dim < ROPE_DIMS), s, 0.0)


def _rope_tables(pos_lanes, freq_lanes, *, tm):
    m = pos_lanes.shape[0]
    spec = pl.BlockSpec((tm, LANES), lambda i: (i, 0))
    return pl.pallas_call(
        _rope_table_body,
        grid=(m // tm,),
        in_specs=[spec, pl.BlockSpec((1, LANES), lambda i: (0, 0))],
        out_specs=[spec, spec, spec],
        out_shape=[jax.ShapeDtypeStruct((m, LANES), F32)] * 3,
        compiler_params=_params("parallel"),
        name="rope_tables",
    )(pos_lanes, freq_lanes)


def _fold_lanes(x, op):
    out = x[:, :LANES]
    for g in range(1, x.shape[1] // LANES):
        out = op(out, x[:, g * LANES:(g + 1) * LANES])
    return out


def _mixers_body(lam_ref, gsub_ref, cos_ref, sin_lo_ref, sin_hi_ref, qa_ref, ka_ref, va_ref,
                 qc_ref, kc_ref, vc_ref, oa_ref, oc_ref, qr_ref, kr_ref, s_ref, p_ref, a_ref,
                 da_ref, sb_ref, *, tq, lam_init):
    nq = qa_ref.shape[1] // tq
    log2e = math.log2(math.e)
    row = lax.broadcasted_iota(jnp.int32, (tq, tq), 0)
    col = lax.broadcasted_iota(jnp.int32, (tq, tq), 1)
    lane = lax.broadcasted_iota(jnp.int32, (tq, LANES), 1)
    half_lanes = (lane < LANES // 2, lane >= LANES // 2)

    half = ROPE_DIMS // 2
    cos, sin_lo, sin_hi = cos_ref[0], sin_lo_ref[0], sin_hi_ref[0]

    def rot(x):
        return (x * cos + pltpu.roll(x, LANES - half, 1) * sin_lo
                + pltpu.roll(x, half, 1) * sin_hi)

    qr_ref[...] = (rot(qa_ref[0].astype(F32)) * (DA_QK ** -0.5 * log2e)).astype(BF16)
    kr_ref[...] = rot(ka_ref[0].astype(F32)).astype(BF16)
    lam_p = lam_ref[...]
    lam = (jnp.exp(jnp.sum(lam_p[0:1] * lam_p[1:2], axis=-1, keepdims=True))
           - jnp.exp(jnp.sum(lam_p[2:3] * lam_p[3:4], axis=-1, keepdims=True)) + lam_init)
    gsub = gsub_ref[...]
    chunk_mask = col // CHUNK <= row // CHUNK

    def diff_sweep(c, hf):
        buf = c % 2
        rows = slice(hf * tq, (hf + 1) * tq)
        q = qr_ref[c * tq:(c + 1) * tq, :]
        qs = jnp.where(half_lanes[hf], q, jnp.zeros_like(q))
        mx = None
        for j in range(c + 1):
            ks = slice(j * tq, (j + 1) * tq)
            s = lax.dot_general(qs, kr_ref[ks, :], _NT, preferred_element_type=F32)
            if j == c:
                s = jnp.where(chunk_mask, s, -jnp.inf)
            s_ref[buf, rows, ks] = s
            mj = _fold_lanes(s, jnp.maximum)
            mx = mj if mx is None else jnp.maximum(mx, mj)
            yield
        m = jnp.max(mx, axis=-1, keepdims=True)
        ls = None
        for j in range(c + 1):
            ks = slice(j * tq, (j + 1) * tq)
            p = jnp.exp2(s_ref[buf, rows, ks] - m)
            p_ref[buf, rows, ks] = p.astype(BF16)
            pj = _fold_lanes(p, jnp.add)
            ls = pj if ls is None else ls + pj
            yield
        l = jnp.sum(ls, axis=-1, keepdims=True)
        n = (c + 1) * tq
        acc = jnp.dot(p_ref[buf, rows, :n], va_ref[0, :n, :], preferred_element_type=F32)
        da_ref[hf] = acc / l

    later = jnp.where(jnp.concatenate([row, row], axis=0) > jnp.concatenate([col, col], axis=0),
                      1.0, 0.0).astype(BF16)
    strictly_before = col < row

    def sb_sweep(c, hf):
        buf = c % 2
        rows = slice(hf * tq, (hf + 1) * tq)
        q = (qc_ref[0, c * tq:(c + 1) * tq, :].astype(F32) * (SB_DIM ** -0.5 * log2e)).astype(BF16)
        qs = jnp.where(half_lanes[hf], q, jnp.zeros_like(q))
        carry = None
        for j in range(c, -1, -1):
            ks = slice(j * tq, (j + 1) * tq)
            z = lax.dot_general(qs, kc_ref[0, ks, :], _NT, preferred_element_type=F32)
            softplus = jnp.log(1.0 + jnp.exp2(-jnp.abs(z))) * log2e
            log_beta = jnp.minimum(z, 0.0) - softplus
            log_1m = log_beta - z
            if j == c:
                log_1m = jnp.where(strictly_before, log_1m, 0.0)
            hi = log_1m.astype(BF16)
            lo = (log_1m - hi.astype(F32)).astype(BF16)
            tail = jnp.dot(jnp.concatenate([hi, lo], axis=1), later,
                           preferred_element_type=F32)
            expo = log_beta + tail
            if carry is not None:
                expo = expo + carry
            a = jnp.exp2(expo)
            if j == c:
                a = jnp.where(strictly_before, a, 0.0)
            a_ref[buf, rows, ks] = a.astype(BF16)
            rs = jnp.sum(_fold_lanes(log_1m, jnp.add), axis=-1, keepdims=True)
            carry = rs if carry is None else carry + rs
            yield
        n = (c + 1) * tq
        sb_ref[hf] = jnp.dot(a_ref[buf, rows, :n], vc_ref[0, :n, :], preferred_element_type=F32)

    for c in range(nq):
        sweeps = [diff_sweep(c, 0), sb_sweep(c, 0), diff_sweep(c, 1), sb_sweep(c, 1)]
        while sweeps:
            for sweep in list(sweeps):
                if next(sweep, "done") == "done":
                    sweeps.remove(sweep)
        out_rows = slice(c * tq, (c + 1) * tq)
        o = da_ref[0] - lam * da_ref[1]
        oa_ref[0, out_rows, :] = (_rms(o, gsub) * (1.0 - lam_init)).astype(BF16)
        oc_ref[0, out_rows, :] = jnp.where(half_lanes[0], sb_ref[0], sb_ref[1]).astype(BF16)


def _mixers(u, rope, lam_p, gsub, *, tq, lam_init):
    b, s, _ = u.shape
    assert DA_HEADS == SB_WIDTH // LANES

    def slab(offset):
        return pl.BlockSpec((1, s, LANES), lambda bi, hi: (bi, 0, offset // LANES + hi))

    table = pl.BlockSpec((1, s, LANES), lambda bi, hi: (bi, 0, 0))
    return pl.pallas_call(
        functools.partial(_mixers_body, tq=tq, lam_init=lam_init),
        grid=(b, DA_HEADS),
        in_specs=[
            pl.BlockSpec((4, DA_QK), lambda bi, hi: (0, 0)),
            pl.BlockSpec((1, DA_V), lambda bi, hi: (0, 0)),
            table, table, table,
            slab(U_QA), slab(U_KA), slab(U_VA), slab(U_QC), slab(U_KC), slab(U_VC),
        ],
        out_specs=[slab(0), slab(0)],
        out_shape=[jax.ShapeDtypeStruct((b, s, DA_WIDTH), BF16),
                   jax.ShapeDtypeStruct((b, s, SB_WIDTH), BF16)],
        scratch_shapes=[pltpu.VMEM((s, LANES), BF16), pltpu.VMEM((s, LANES), BF16),
                        pltpu.VMEM((2, 2 * tq, s), F32), pltpu.VMEM((2, 2 * tq, s), BF16),
                        pltpu.VMEM((2, 2 * tq, s), BF16),
                        pltpu.VMEM((2, tq, DA_V), F32), pltpu.VMEM((2, tq, LANES), F32)],
        compiler_params=_params("parallel", "parallel"),
        name="mixers",
    )(lam_p, gsub, *rope, u, u, u, u, u, u)


def _gla_body(gate_ref, wup_ref, bup_ref, gnorm_ref, q_ref, k_ref, v_ref, og_ref, o_ref,
              loga_ref, state_ref):
    s = q_ref.shape[1]
    pre = jnp.dot(gate_ref[0].astype(BF16), wup_ref[...], preferred_element_type=F32)
    loga_ref[...] = _log_sigmoid(pre + bup_ref[...]) * (1.0 / GLA_GATE_NORM)
    state_ref[...] = jnp.zeros_like(state_ref)

    rr = lax.broadcasted_iota(jnp.int32, (CHUNK, CHUNK), 0)
    cc = lax.broadcasted_iota(jnp.int32, (CHUNK, CHUNK), 1)
    causal = rr >= cc
    prefix = jnp.where(causal, 1.0, 0.0).astype(BF16)
    gnorm = gnorm_ref[...]

    def chunk(n):
        sl = pl.ds(pl.multiple_of(n * CHUNK, CHUNK), CHUNK)
        la = loga_ref[sl, :]
        hi = la.astype(BF16)
        r1 = la - hi.astype(F32)
        mid = r1.astype(BF16)
        lo = (r1 - mid.astype(F32)).astype(BF16)
        bcum = (jnp.dot(prefix, hi, preferred_element_type=F32)
                + jnp.dot(prefix, mid, preferred_element_type=F32)
                + jnp.dot(prefix, lo, preferred_element_type=F32))
        b_last = bcum[CHUNK - 1:CHUNK, :]
        q = q_ref[0, sl, :].astype(F32) * GLA_DK ** -0.5
        k = k_ref[0, sl, :].astype(F32)
        q_d = (q * jnp.exp(bcum)).astype(BF16)
        k_d = (k * jnp.exp(-bcum)).astype(BF16)
        k_l = (k * jnp.exp(b_last - bcum)).astype(BF16)
        decay = jnp.exp(b_last)
        yield

        def head(hd):
            ks = slice(hd * GLA_DK, (hd + 1) * GLA_DK)
            vs = slice(hd * GLA_DV, (hd + 1) * GLA_DV)
            v = v_ref[0, sl, vs]
            att = lax.dot_general(q_d[:, ks], k_d[:, ks], _NT, preferred_element_type=F32)
            att = jnp.where(causal, att, 0.0).astype(BF16)
            yield
            st = state_ref[hd]
            o = (jnp.dot(att, v, preferred_element_type=F32)
                 + lax.dot_general(q_d[:, ks], st.astype(BF16), _NT, preferred_element_type=F32))
            yield
            state_ref[hd] = decay[:, ks] * st + lax.dot_general(
                v, k_l[:, ks], _TN, preferred_element_type=F32)
            yield
            og = og_ref[0, sl, vs].astype(F32)
            o_ref[0, sl, vs] = (_rms(o, gnorm) * (og * _sigmoid(og))).astype(BF16)

        heads = [head(hd) for hd in range(GLA_HEADS)]
        while heads:
            for h in list(heads):
                if next(h, "done") == "done":
                    heads.remove(h)

    group = math.gcd(GLA_CHUNKS_PER_TRIP, s // CHUNK)

    def trip(g, _):
        chunks = [chunk(g * group + t) for t in range(group)]
        for ch in chunks:
            next(ch)
        for ch in chunks:
            for _ in ch:
                pass
        return 0

    lax.fori_loop(0, s // CHUNK // group, trip, 0)


def _gla(u, gate, wup, bup, gnorm):
    b, s, _ = u.shape
    hk = GLA_HEADS * GLA_DK
    return pl.pallas_call(
        _gla_body,
        grid=(b,),
        in_specs=[
            pl.BlockSpec((1, s, LANES), lambda bi: (bi, 0, 0)),
            pl.BlockSpec((LANES, hk), lambda bi: (0, 0)),
            pl.BlockSpec((1, hk), lambda bi: (0, 0)),
            pl.BlockSpec((1, GLA_DV), lambda bi: (0, 0)),
            pl.BlockSpec((1, s, hk), lambda bi: (bi, 0, U_QB // hk)),
            pl.BlockSpec((1, s, hk), lambda bi: (bi, 0, U_KB // hk)),
            pl.BlockSpec((1, s, GLA_WIDTH), lambda bi: (bi, 0, U_VB // GLA_WIDTH)),
            pl.BlockSpec((1, s, GLA_WIDTH), lambda bi: (bi, 0, U_OB // GLA_WIDTH)),
        ],
        out_specs=pl.BlockSpec((1, s, GLA_WIDTH), lambda bi: (bi, 0, 0)),
        out_shape=jax.ShapeDtypeStruct((b, s, GLA_WIDTH), BF16),
        scratch_shapes=[pltpu.VMEM((s, hk), F32), pltpu.VMEM((GLA_HEADS, GLA_DV, GLA_DK), F32)],
        compiler_params=_params("parallel"),
        name="gla",
    )(gate, wup, bup, gnorm, u, u, u, u)


def _outproj_body(h_ref, oa_ref, ob_ref, oc_ref, w_ref, o_ref):
    b0, c0 = DA_WIDTH, DA_WIDTH + GLA_WIDTH
    o_ref[...] = (h_ref[...]
                  + jnp.dot(oa_ref[...], w_ref[:b0, :], preferred_element_type=F32)
                  + jnp.dot(ob_ref[...], w_ref[b0:c0, :], preferred_element_type=F32)
                  + jnp.dot(oc_ref[...], w_ref[c0:, :], preferred_element_type=F32))


def _outproj(h, oa, ob, oc, w, *, tm):
    m, d = h.shape

    def rows(width):
        return pl.BlockSpec((tm, width), lambda i: (i, 0))

    return pl.pallas_call(
        _outproj_body,
        grid=(m // tm,),
        in_specs=[rows(d), rows(DA_WIDTH), rows(GLA_WIDTH), rows(SB_WIDTH),
                  pl.BlockSpec(w.shape, lambda i: (0, 0))],
        out_specs=rows(d),
        out_shape=jax.ShapeDtypeStruct((m, d), F32),
        compiler_params=_params("parallel"),
        name="outproj",
    )(h, oa, ob, oc, w)


def _to_bf16_body(w_ref, o_ref):
    o_ref[...] = w_ref[0].astype(BF16)


def _layer_to_bf16(w, layer, *, br):
    _, r, c = w.shape
    assert r % br == 0
    return pl.pallas_call(
        _to_bf16_body,
        grid=(r // br,),
        in_specs=[pl.BlockSpec((1, br, c), lambda i: (layer, i, 0))],
        out_specs=pl.BlockSpec((br, c), lambda i: (i, 0)),
        out_shape=jax.ShapeDtypeStruct((r, c), BF16),
        compiler_params=_params("parallel"),
        name="to_bf16",
    )(w)


def _ple_body(h_ref, p_ref, g_ref, wg_ref, wp_ref, gf_ref, o_ref, *, final):
    x = h_ref[...]
    n = _rms(x, g_ref[...]).astype(BF16)
    gate = _sigmoid(jnp.dot(n, wg_ref[...], preferred_element_type=F32))
    emb = jnp.dot(p_ref[...].astype(BF16), wp_ref[...], preferred_element_type=F32)
    y = x + emb * gate
    o_ref[...] = _rms(y, gf_ref[...]) if final else y


def _ple(h, p, g, wg, wp, gf, *, tm, final):
    m, d = h.shape
    pd = p.shape[1]
    return pl.pallas_call(
        functools.partial(_ple_body, final=final),
        grid=(m // tm,),
        in_specs=[
            pl.BlockSpec((tm, d), lambda i: (i, 0)),
            pl.BlockSpec((tm, pd), lambda i: (i, 0)),
            pl.BlockSpec((1, d), lambda i: (0, 0)),
            pl.BlockSpec((d, d), lambda i: (0, 0)),
            pl.BlockSpec((pd, d), lambda i: (0, 0)),
            pl.BlockSpec((1, d), lambda i: (0, 0)),
        ],
        out_specs=pl.BlockSpec((tm, d), lambda i: (i, 0)),
        out_shape=jax.ShapeDtypeStruct((m, d), F32),
        compiler_params=_params("parallel"),
        name="ple",
    )(h, p, g, wg, wp, gf)


def _reorder_w_in(w):
    sizes = (DA_WIDTH, DA_WIDTH, DA_WIDTH, GLA_HEADS * GLA_DK, GLA_HEADS * GLA_DK, GLA_WIDTH,
             GLA_RANK, GLA_WIDTH, SB_WIDTH, SB_WIDTH, SB_WIDTH)
    parts, start = [], 0
    for size in sizes:
        parts.append(w[:, start:start + size])
        start += size
    qa, ka, va, qb, kb, vb, gb, ob, qc, kc, vc = parts
    main = jnp.concatenate([qa, ka, va, qb, vb, ob, kb, qc, kc, vc], axis=1).astype(BF16)
    gate = jnp.pad(gb, ((0, 0), (0, LANES - GLA_RANK))).astype(BF16)
    return main, gate


def kernel(x, p, positions, g_ffn1, w_ffn1_gate, w_ffn1_up, w_ffn1_down, g_mix, w_in, w_gla_gate_up, b_gla_gate, g_gla_norm, lambda_q1, lambda_k1, lambda_q2, lambda_k2, g_diff_norm, w_out, g_ffn2, w_ffn2_gate, w_ffn2_up, w_ffn2_down, g_ple, w_ple_gate, w_ple_proj, g_final):
    b, s, d = x.shape
    m = b * s
    tm = min(512, m)
    tm_big = min(1024, m)
    tq = min(256, s)
    assert m % tm_big == 0 and m % tm == 0 and s % tq == 0 and tq % CHUNK == 0

    inv_freq = ROPE_THETA ** (-jnp.arange(0, ROPE_DIMS, 2, dtype=F32) / ROPE_DIMS)
    freq_lanes = jnp.tile(inv_freq, LANES // inv_freq.shape[0]).reshape(1, LANES)
    pos_lanes = jnp.broadcast_to(positions.astype(F32).reshape(m, 1), (m, LANES))
    rope_c, rope_s1, rope_s2 = _rope_tables(pos_lanes, freq_lanes, tm=tm)
    rope = [t.reshape(b, s, LANES) for t in (rope_c, rope_s1, rope_s2)]

    def ffn_weights(wg, wu, wd, i):
        return (_layer_to_bf16(wg, i, br=256), _layer_to_bf16(wu, i, br=256),
                _layer_to_bf16(wd, i, br=512))

    h = x.reshape(m, d)
    for i in range(DEPTH):
        row = lambda a: a[i].reshape(1, -1)
        h = _ffn(h, row(g_ffn1), *ffn_weights(w_ffn1_gate, w_ffn1_up, w_ffn1_down, i),
                 tm=tm_big, tf=512)

        w_main, w_gate = _reorder_w_in(w_in[i])
        u, gate_low = _inproj(h, row(g_mix), w_main, w_gate, tm=tm_big, tn=2048)
        u3 = u.reshape(b, s, U_WIDTH)

        lam_init = 0.8 - 0.6 * math.exp(-0.3 * i)
        lam_p = jnp.stack([lambda_q1[i], lambda_k1[i], lambda_q2[i], lambda_k2[i]])
        o_a, o_c = _mixers(u3, rope, lam_p, row(g_diff_norm), tq=tq, lam_init=lam_init)
        wup = jnp.pad(w_gla_gate_up[i], ((0, LANES - GLA_RANK), (0, 0))).astype(BF16)
        o_b = _gla(u3, gate_low.reshape(b, s, LANES), wup, row(b_gla_gate), row(g_gla_norm))

        h = _outproj(h, o_a.reshape(m, DA_WIDTH), o_b.reshape(m, GLA_WIDTH),
                     o_c.reshape(m, SB_WIDTH), _layer_to_bf16(w_out, i, br=512), tm=tm)

        h = _ffn(h, row(g_ffn2), *ffn_weights(w_ffn2_gate, w_ffn2_up, w_ffn2_down, i),
                 tm=tm_big, tf=512)
        h = _ple(h, p[i].reshape(m, PLE_DIM), row(g_ple), _layer_to_bf16(w_ple_gate, i, br=512),
                 w_ple_proj[i].astype(BF16), g_final.reshape(1, -1), tm=tm,
                 final=(i == DEPTH - 1))
    return h.reshape(b, s, d)
```
